```python
import math
import jax, jax.numpy as jnp
from jax import lax
import numpy as np

D_MODEL = 1024
BATCH = 4
SEQ = 8192
DEPTH = 1
DEC_BATCH = 32
DEC_SEQ = 32
PAST_LEN = 2048

CHUNK = 64
N_HEADS = 8
HEAD_DIM = 64
SB_WIDTH = N_HEADS * HEAD_DIM
CONV_WIDTH = 512
CONV_K = 3
PLE_DIM = 256
D_FF = 4 * D_MODEL
Q_BLOCK = 128
EPS = 1e-6
SPLITS = tuple(int(s) for s in np.cumsum([SB_WIDTH, SB_WIDTH, SB_WIDTH, CONV_WIDTH, CONV_WIDTH, CONV_WIDTH, D_MODEL]))
N_PROJ = 3 * SB_WIDTH + 3 * CONV_WIDTH + 2 * D_MODEL

kernel_name = "stick_breaking_shortconv_griffin_step"


def rmsnorm(x, g):
    xf = x.astype(jnp.float32)
    y = xf * lax.rsqrt(jnp.mean(xf * xf, axis=-1, keepdims=True) + EPS)
    return y.astype(x.dtype) * g


def stick_breaking(q, k, v, q_pos, k_pos):
    scale = 1.0 / math.sqrt(HEAD_DIM)
    z = jnp.einsum('bqhd,bkhd->bhqk', q.astype(jnp.float32), k.astype(jnp.float32)) * scale
    mask = k_pos[None, :] < q_pos[:, None]
    log_1m = jnp.where(mask, jax.nn.log_sigmoid(-z), 0.0)
    tail = lax.cumsum(log_1m, axis=3, reverse=True) - log_1m
    a = jnp.where(mask, jnp.exp(jax.nn.log_sigmoid(z) + tail), 0.0)
    out = jnp.einsum('bhqk,bkhd->bqhd', a, v.astype(jnp.float32))
    return out.astype(v.dtype)


def sb_prompt(q, k, v):
    b, t, h, d = q.shape
    nb = t // Q_BLOCK
    qb = q.reshape(b, nb, Q_BLOCK, h, d).swapaxes(0, 1)
    pos = jnp.arange(t, dtype=jnp.int32).reshape(nb, Q_BLOCK)
    kpos = jnp.arange(t, dtype=jnp.int32)
    out = lax.map(lambda a: stick_breaking(a[0], k, v, a[1], kpos), (qb, pos))
    return out.swapaxes(0, 1).reshape(b, t, h, d)


def causal_conv(u, buf, w):
    t = u.shape[1]
    up = jnp.concatenate([buf, u], axis=1)
    y = sum(w[j] * up[:, j:j + t] for j in range(CONV_K))
    return y, up[:, -(CONV_K - 1):]


def layer(x, p, attend, conv_buf, g_mix, w_in, conv_w, w_attn_out, w_conv_out, w_o,
          g_ffn, w_up, w_down, g_ple, w_ple_gate, w_ple):
    b, t, _ = x.shape
    h = rmsnorm(x, g_mix)
    proj = h @ w_in
    q, k, v, cb, cc, cx, ga, gc = jnp.split(proj, SPLITS, axis=-1)
    q = q.reshape(b, t, N_HEADS, HEAD_DIM)
    k = k.reshape(b, t, N_HEADS, HEAD_DIM)
    v = v.reshape(b, t, N_HEADS, HEAD_DIM)
    y_attn = attend(q, k, v).reshape(b, t, SB_WIDTH) @ w_attn_out
    conv_out, new_buf = causal_conv(cc * cx, conv_buf, conv_w)
    y_conv = (cb * conv_out) @ w_conv_out
    merged = jax.nn.sigmoid(ga) * y_attn + jax.nn.sigmoid(gc) * y_conv
    x = x + merged @ w_o
    f = jnp.square(jax.nn.relu(rmsnorm(x, g_ffn) @ w_up)) @ w_down
    x = x + f
    x = x + jax.nn.sigmoid(rmsnorm(x, g_ple) @ w_ple_gate) * (p @ w_ple)
    return x, k, v, new_buf


def setup_inputs(seed: int = 0) -> dict:
    key = jax.random.key(seed)
    ks = jax.random.split(key, 24)
    f32 = jnp.float32
    n = lambda k, shape, s: jax.random.normal(k, shape, f32) * s
    return {
        "x_prompt": n(ks[0], (BATCH, SEQ, D_MODEL), 1.0),
        "x_sample": n(ks[1], (DEC_BATCH, DEC_SEQ, D_MODEL), 1.0),
        "p_prompt": n(ks[2], (DEPTH, BATCH, SEQ, PLE_DIM), 1.0),
        "p_sample": n(ks[3], (DEPTH, DEC_BATCH, DEC_SEQ, PLE_DIM), 1.0),
        "cache_k": n(ks[4], (DEPTH, DEC_BATCH, PAST_LEN, N_HEADS, HEAD_DIM), 1.0),
        "cache_v": n(ks[5], (DEPTH, DEC_BATCH, PAST_LEN, N_HEADS, HEAD_DIM), 1.0),
        "cache_conv": n(ks[6], (DEPTH, DEC_BATCH, CONV_K - 1, CONV_WIDTH), 1.0),
        "g_mix": 1.0 + n(ks[7], (DEPTH, D_MODEL), 0.02),
        "w_in": n(ks[8], (DEPTH, D_MODEL, N_PROJ), D_MODEL ** -0.5),
        "conv_w": n(ks[9], (DEPTH, CONV_K, CONV_WIDTH), CONV_K ** -0.5),
        "w_attn_out": n(ks[10], (DEPTH, SB_WIDTH, D_MODEL), SB_WIDTH ** -0.5),
        "w_conv_out": n(ks[11], (DEPTH, CONV_WIDTH, D_MODEL), CONV_WIDTH ** -0.5),
        "w_o": n(ks[12], (DEPTH, D_MODEL, D_MODEL), D_MODEL ** -0.5),
        "g_ffn": 1.0 + n(ks[13], (DEPTH, D_MODEL), 0.02),
        "w_up": n(ks[14], (DEPTH, D_MODEL, D_FF), D_MODEL ** -0.5),
        "w_down": n(ks[15], (DEPTH, D_FF, D_MODEL), D_FF ** -0.5),
        "g_ple": 1.0 + n(ks[16], (DEPTH, D_MODEL), 0.02),
        "w_ple_gate": n(ks[17], (DEPTH, D_MODEL, D_MODEL), D_MODEL ** -0.5),
        "w_ple": n(ks[18], (DEPTH, PLE_DIM, D_MODEL), PLE_DIM ** -0.5),
        "g_final": 1.0 + n(ks[19], (D_MODEL,), 0.02),
    }


def reference(x_prompt, x_sample, p_prompt, p_sample, cache_k, cache_v, cache_conv,
              g_mix, w_in, conv_w, w_attn_out, w_conv_out, w_o,
              g_ffn, w_up, w_down, g_ple, w_ple_gate, w_ple, g_final):
    xp, xs = x_prompt, x_sample
    kp_l, vp_l, cp_l, ks_l, vs_l, cs_l = [], [], [], [], [], []
    for i in range(DEPTH):
        w = (g_mix[i], w_in[i], conv_w[i], w_attn_out[i], w_conv_out[i], w_o[i],
             g_ffn[i], w_up[i], w_down[i], g_ple[i], w_ple_gate[i], w_ple[i])
        buf0 = jnp.zeros((xp.shape[0], CONV_K - 1, CONV_WIDTH), xp.dtype)
        xp, kp, vp, cp = layer(xp, p_prompt[i], sb_prompt, buf0, *w)
        ck, cv = cache_k[i], cache_v[i]
        past = ck.shape[1]

        def sb_sample(q, k, v, ck=ck, cv=cv, past=past):
            t = q.shape[1]
            kk = jnp.concatenate([ck, k], axis=1)
            vv = jnp.concatenate([cv, v], axis=1)
            q_pos = past + jnp.arange(t, dtype=jnp.int32)
            k_pos = jnp.arange(past + t, dtype=jnp.int32)
            return stick_breaking(q, kk, vv, q_pos, k_pos)

        xs, ks_, vs_, cs = layer(xs, p_sample[i], sb_sample, cache_conv[i], *w)
        kp_l.append(kp); vp_l.append(vp); cp_l.append(cp)
        ks_l.append(ks_); vs_l.append(vs_); cs_l.append(cs)
    y_prompt = rmsnorm(xp, g_final)
    y_sample = rmsnorm(xs, g_final)
    return (y_prompt, y_sample,
            jnp.stack(kp_l), jnp.stack(vp_l), jnp.stack(cp_l),
            jnp.stack(ks_l), jnp.stack(vs_l), jnp.stack(cs_l))
```

```python
import functools
import math

import jax
import jax.numpy as jnp
from jax import lax
from jax.experimental import pallas as pl
from jax.experimental.pallas import tpu as pltpu

F32 = jnp.float32
BF16 = jnp.bfloat16

D_MODEL = 1024
N_HEADS = 8
HEAD_DIM = 64
SB_WIDTH = N_HEADS * HEAD_DIM
CONV_WIDTH = 512
CONV_K = 3
PLE_DIM = 256
D_FF = 4 * D_MODEL
EPS = 1e-6
N_PROJ = 3 * SB_WIDTH + 3 * CONV_WIDTH + 2 * D_MODEL

LANES = 128
HEADS_PER_TILE = LANES // HEAD_DIM
N_PAIRS = N_HEADS // HEADS_PER_TILE
ROW_TILE = 512
ATT_BLOCK = 256
FF_CHUNK = 1024
CONV_PAD = 8
LOG_ZERO = -104.0
VMEM_LIMIT = 52 * 1024 * 1024


def _resident(shape):
    nd = len(shape)
    return pl.BlockSpec(shape, lambda *_: (0,) * nd, pipeline_mode=pl.Buffered(1))


def _rms(x, g):
    return (x * lax.rsqrt(jnp.mean(x * x, axis=-1, keepdims=True) + EPS)) * g


def _dot(a, b):
    return jnp.dot(a, b, preferred_element_type=F32)


def _proj_kernel(*refs, tiles_per_seq, seq_in_tile):
    if seq_in_tile is None:
        (x_ref, g_ref, w_ref, cw_ref,
         k_out, v_out, q_bf, k_bf, v_bf, cpre, sga, sgc, cstate, s_ref) = refs
    else:
        (x_ref, g_ref, w_ref, cw_ref, e1_ref, e2_ref,
         k_out, v_out, q_bf, k_bf, v_bf, cpre, sga, sgc, cstate, s_ref) = refs
    tm = x_ref.shape[0]
    hb = _rms(x_ref[...], g_ref[...]).astype(BF16)

    def proj(lo, hi):
        return _dot(hb, w_ref[:, lo:hi])

    o = 0
    q_bf[...] = (proj(o, o + SB_WIDTH) * (1.0 / math.sqrt(HEAD_DIM))).astype(BF16)
    o += SB_WIDTH
    k = proj(o, o + SB_WIDTH)
    k_out[...] = k
    k_bf[...] = k.astype(BF16)
    o += SB_WIDTH
    v = proj(o, o + SB_WIDTH)
    v_out[...] = v
    v_bf[...] = v.astype(BF16)
    o += SB_WIDTH
    cb = proj(o, o + CONV_WIDTH)
    o += CONV_WIDTH
    cc = proj(o, o + CONV_WIDTH)
    o += CONV_WIDTH
    cx = proj(o, o + CONV_WIDTH)
    o += CONV_WIDTH
    u = cc * cx

    if seq_in_tile is None:
        @pl.when(pl.program_id(0) % tiles_per_seq == 0)
        def _():
            s_ref[0:CONV_PAD, :] = jnp.zeros((CONV_PAD, CONV_WIDTH), F32)
    else:
        s_ref[0:CONV_PAD, :] = jnp.zeros((CONV_PAD, CONV_WIDTH), F32)
    s_ref[CONV_PAD:CONV_PAD + tm, :] = u
    prev1 = s_ref[CONV_PAD - 1:CONV_PAD - 1 + tm, :]
    prev2 = s_ref[CONV_PAD - 2:CONV_PAD - 2 + tm, :]
    if seq_in_tile is None:
        s_ref[CONV_PAD - 2:CONV_PAD, :] = u[tm - 2:tm, :]
        cstate[0] = u[tm - 2:tm, :]
    else:
        r = lax.broadcasted_iota(jnp.int32, (tm, 1), 0) % seq_in_tile
        prev1 = jnp.where(r < 1, e1_ref[...], prev1)
        prev2 = jnp.where(r < 2, e2_ref[...], prev2)
        cstate[...] = u
    conv = cw_ref[0:1, :] * prev2 + cw_ref[1:2, :] * prev1 + cw_ref[2:3, :] * u
    cpre[...] = (cb * conv).astype(BF16)

    sga[...] = jax.nn.sigmoid(proj(o, o + D_MODEL)).astype(BF16)
    o += D_MODEL
    sgc[...] = jax.nn.sigmoid(proj(o, o + D_MODEL)).astype(BF16)


def _proj_call(x, g, w_in, conv_w, seq_len, hist=None):
    n = x.shape[0]
    tm = min(ROW_TILE, n)
    grid = (n // tm,)
    sample = hist is not None
    row = lambda w: pl.BlockSpec((tm, w), lambda t: (t, 0))
    in_specs = [row(D_MODEL), _resident((1, D_MODEL)), _resident((D_MODEL, N_PROJ)),
                _resident((CONV_K, CONV_WIDTH))]
    args = [x, g.reshape(1, D_MODEL), w_in, conv_w]
    if sample:
        in_specs += [row(CONV_WIDTH), row(CONV_WIDTH)]
        args += list(hist)
        tiles_per_seq, seq_in_tile = None, seq_len
        cstate_shape = jax.ShapeDtypeStruct((n, CONV_WIDTH), F32)
        cstate_spec = row(CONV_WIDTH)
    else:
        tiles_per_seq, seq_in_tile = seq_len // tm, None
        cstate_shape = jax.ShapeDtypeStruct((n // seq_len, CONV_K - 1, CONV_WIDTH), F32)
        cstate_spec = pl.BlockSpec((1, CONV_K - 1, CONV_WIDTH),
                                   lambda t: (t // tiles_per_seq, 0, 0))
    sds = jax.ShapeDtypeStruct
    out_shape = (sds((n, SB_WIDTH), F32), sds((n, SB_WIDTH), F32),
                 sds((n, SB_WIDTH), BF16), sds((n, SB_WIDTH), BF16), sds((n, SB_WIDTH), BF16),
                 sds((n, CONV_WIDTH), BF16), sds((n, D_MODEL), BF16), sds((n, D_MODEL), BF16),
                 cstate_shape)
    out_specs = (row(SB_WIDTH), row(SB_WIDTH), row(SB_WIDTH), row(SB_WIDTH), row(SB_WIDTH),
                 row(CONV_WIDTH), row(D_MODEL), row(D_MODEL), cstate_spec)
    return pl.pallas_call(
        functools.partial(_proj_kernel, tiles_per_seq=tiles_per_seq, seq_in_tile=seq_in_tile),
        grid=grid, in_specs=in_specs, out_specs=out_specs, out_shape=out_shape,
        scratch_shapes=[pltpu.VMEM((tm + CONV_PAD, CONV_WIDTH), F32)],
        compiler_params=pltpu.CompilerParams(
            dimension_semantics=("arbitrary",), vmem_limit_bytes=VMEM_LIMIT),
        name="proj_sample" if sample else "proj_prompt",
    )(*args)


def _sb_block(qm, kw, vw, tri, carry_ref, acc_ref, valid):
    z = lax.dot_general(qm, kw, (((1,), (1,)), ((), ())), preferred_element_type=F32)
    sp = jnp.maximum(z, 0.0) + jnp.log(1.0 + jnp.exp(-jnp.abs(z)))
    if valid is not None:
        sp = jnp.where(valid, sp, 0.0)
    hi = sp.astype(BF16)
    lo = (sp - hi.astype(F32)).astype(BF16)
    tail = _dot(hi, tri) + _dot(lo, tri)
    carry = carry_ref[...]
    a = jnp.exp((z - sp) - tail + carry)
    if valid is not None:
        a = jnp.where(valid, a, 0.0)
    acc_ref[...] += _dot(a.astype(BF16), vw)
    carry = carry - jnp.sum(sp, axis=1, keepdims=True)
    carry_ref[...] = carry
    return jnp.max(carry)


def _lane_half(half):
    lane = lax.broadcasted_iota(jnp.int32, (1, LANES), 1)
    return (lane < HEAD_DIM) if half == 0 else (lane >= HEAD_DIM)


def _attn_prompt_kernel(q_ref, k_ref, v_ref, tri_ref, o_ref, carry_ref, acc_ref):
    i = pl.program_id(2)
    tq = q_ref.shape[0]
    rows = lax.broadcasted_iota(jnp.int32, (tq, tq), 0)
    cols = lax.broadcasted_iota(jnp.int32, (tq, tq), 1)
    diag_valid = cols < rows
    for half in range(HEADS_PER_TILE):
        qm = jnp.where(_lane_half(half), q_ref[...], jnp.zeros((), BF16))
        carry_ref[...] = jnp.zeros_like(carry_ref)
        acc_ref[half] = jnp.zeros((tq, LANES), F32)
        k0 = pl.multiple_of(i * tq, tq)
        mx = _sb_block(qm, k_ref[pl.ds(k0, tq), :], v_ref[pl.ds(k0, tq), :], tri_ref[...],
                       carry_ref, acc_ref.at[half], diag_valid)

        def body(state, qm=qm, half=half):
            j, _ = state
            k0 = pl.multiple_of(j * tq, tq)
            mx = _sb_block(qm, k_ref[pl.ds(k0, tq), :], v_ref[pl.ds(k0, tq), :], tri_ref[...],
                           carry_ref, acc_ref.at[half], None)
            return j - 1, mx

        lax.while_loop(lambda s: (s[0] >= 0) & (s[1] > LOG_ZERO), body, (i - 1, mx))
    o_ref[...] = jnp.where(_lane_half(0), acc_ref[0], acc_ref[1]).astype(BF16)


def _attn_prompt_call(q_bf, k_bf, v_bf, tri, batch, seq):
    tq = ATT_BLOCK
    nq = seq // tq
    return pl.pallas_call(
        _attn_prompt_kernel,
        grid=(batch, N_PAIRS, nq),
        in_specs=[pl.BlockSpec((tq, LANES), lambda b, p, i: (b * nq + i, p)),
                  pl.BlockSpec((seq, LANES), lambda b, p, i: (b, p)),
                  pl.BlockSpec((seq, LANES), lambda b, p, i: (b, p)),
                  _resident((tq, tq))],
        out_specs=pl.BlockSpec((tq, LANES), lambda b, p, i: (b * nq + i, p)),
        out_shape=jax.ShapeDtypeStruct((batch * seq, SB_WIDTH), BF16),
        scratch_shapes=[pltpu.VMEM((tq, 1), F32), pltpu.VMEM((HEADS_PER_TILE, tq, LANES), F32)],
        compiler_params=pltpu.CompilerParams(
            dimension_semantics=("arbitrary", "arbitrary", "arbitrary"),
            vmem_limit_bytes=VMEM_LIMIT),
        name="attn_prompt",
    )(q_bf, k_bf, v_bf, tri)


def _attn_sample_kernel(q_ref, kn_ref, vn_ref, ck_ref, cv_ref, tri_ref, o_ref,
                        carry_ref, acc_ref, kpad_ref, vpad_ref):
    tq = q_ref.shape[0]
    past = ck_ref.shape[1]
    tk = tri_ref.shape[0]
    kpad_ref[...] = jnp.zeros_like(kpad_ref)
    vpad_ref[...] = jnp.zeros_like(vpad_ref)
    kpad_ref[0:tq, :] = kn_ref[...]
    vpad_ref[0:tq, :] = vn_ref[...]
    rows = lax.broadcasted_iota(jnp.int32, (tq, LANES), 0)
    cols = lax.broadcasted_iota(jnp.int32, (tq, LANES), 1)
    diag_valid = cols < rows
    for half in range(HEADS_PER_TILE):
        qm = jnp.where(_lane_half(half), q_ref[...], jnp.zeros((), BF16))
        carry_ref[...] = jnp.zeros_like(carry_ref)
        acc_ref[half] = jnp.zeros((tq, LANES), F32)
        mx = _sb_block(qm, kpad_ref[...], vpad_ref[...], tri_ref[0:LANES, 0:LANES],
                       carry_ref, acc_ref.at[half], diag_valid)

        def body(state, qm=qm, half=half):
            j, _ = state
            k0 = pl.multiple_of(j * tk, tk)
            mx = _sb_block(qm, ck_ref[0, pl.ds(k0, tk), :].astype(BF16),
                           cv_ref[0, pl.ds(k0, tk), :].astype(BF16), tri_ref[...],
                           carry_ref, acc_ref.at[half], None)
            return j - 1, mx

        lax.while_loop(lambda s: (s[0] >= 0) & (s[1] > LOG_ZERO), body, (past // tk - 1, mx))
    o_ref[...] = jnp.where(_lane_half(0), acc_ref[0], acc_ref[1]).astype(BF16)


def _attn_sample_call(q_bf, k_bf, v_bf, cache_k, cache_v, tri, batch, seq):
    past = cache_k.shape[1]
    return pl.pallas_call(
        _attn_sample_kernel,
        grid=(batch, N_PAIRS),
        in_specs=[pl.BlockSpec((seq, LANES), lambda b, p: (b, p)),
                  pl.BlockSpec((seq, LANES), lambda b, p: (b, p)),
                  pl.BlockSpec((seq, LANES), lambda b, p: (b, p)),
                  pl.BlockSpec((1, past, LANES), lambda b, p: (b, 0, p)),
                  pl.BlockSpec((1, past, LANES), lambda b, p: (b, 0, p)),
                  _resident(tri.shape)],
        out_specs=pl.BlockSpec((seq, LANES), lambda b, p: (b, p)),
        out_shape=jax.ShapeDtypeStruct((batch * seq, SB_WIDTH), BF16),
        scratch_shapes=[pltpu.VMEM((seq, 1), F32),
                        pltpu.VMEM((HEADS_PER_TILE, seq, LANES), F32),
                        pltpu.VMEM((LANES, LANES), BF16), pltpu.VMEM((LANES, LANES), BF16)],
        compiler_params=pltpu.CompilerParams(
            dimension_semantics=("arbitrary", "arbitrary"), vmem_limit_bytes=VMEM_LIMIT),
        name="attn_sample",
    )(q_bf, k_bf, v_bf, cache_k, cache_v, tri)


def _post_kernel(x_ref, att_ref, cpre_ref, sga_ref, sgc_ref, p_ref,
                 wa_ref, wc_ref, wo_ref, gf_ref, wu_ref, wd_ref, gp_ref, wg_ref, wp_ref, gl_ref,
                 y_ref):
    ya = _dot(att_ref[...], wa_ref[...])
    yc = _dot(cpre_ref[...], wc_ref[...])
    merged = sga_ref[...].astype(F32) * ya + sgc_ref[...].astype(F32) * yc
    x1 = x_ref[...] + _dot(merged.astype(BF16), wo_ref[...])
    h2 = _rms(x1, gf_ref[...]).astype(BF16)
    f = jnp.zeros_like(x1)
    for c in range(0, D_FF, FF_CHUNK):
        up = jnp.maximum(_dot(h2, wu_ref[:, c:c + FF_CHUNK]), 0.0)
        f = f + _dot((up * up).astype(BF16), wd_ref[c:c + FF_CHUNK, :])
    x2 = x1 + f
    h3 = _rms(x2, gp_ref[...]).astype(BF16)
    gate = jax.nn.sigmoid(_dot(h3, wg_ref[...]))
    x3 = x2 + gate * _dot(p_ref[...].astype(BF16), wp_ref[...])
    y_ref[...] = _rms(x3, gl_ref[...])


def _post_call(x, att, cpre, sga, sgc, p, wa, wc, wo, gf, wu, wd, gp, wg, wp, gl, name):
    n = x.shape[0]
    tm = min(ROW_TILE, n)
    row = lambda w: pl.BlockSpec((tm, w), lambda t: (t, 0))
    vec = lambda g: g.reshape(1, D_MODEL)
    return pl.pallas_call(
        _post_kernel,
        grid=(n // tm,),
        in_specs=[row(D_MODEL), row(SB_WIDTH), row(CONV_WIDTH), row(D_MODEL), row(D_MODEL),
                  row(PLE_DIM),
                  _resident(wa.shape), _resident(wc.shape), _resident(wo.shape),
                  _resident((1, D_MODEL)), _resident(wu.shape), _resident(wd.shape),
                  _resident((1, D_MODEL)), _resident(wg.shape), _resident(wp.shape),
                  _resident((1, D_MODEL))],
        out_specs=row(D_MODEL),
        out_shape=jax.ShapeDtypeStruct((n, D_MODEL), F32),
        compiler_params=pltpu.CompilerParams(
            dimension_semantics=("arbitrary",), vmem_limit_bytes=VMEM_LIMIT),
        name=name,
    )(x, att, cpre, sga, sgc, p, wa, wc, wo, vec(gf), wu, wd, vec(gp), wg, wp, vec(gl))


def kernel(x_prompt, x_sample, p_prompt, p_sample, cache_k, cache_v, cache_conv,
           g_mix, w_in, conv_w, w_attn_out, w_conv_out, w_o,
           g_ffn, w_up, w_down, g_ple, w_ple_gate, w_ple, g_final):
    depth = w_in.shape[0]
    assert depth == 1
    b, t, _ = x_prompt.shape
    db, dt, _ = x_sample.shape
    past = cache_k.shape[2]
    bf = lambda w: w[0].astype(BF16)
    w_in_b, wa, wc, wo = bf(w_in), bf(w_attn_out), bf(w_conv_out), bf(w_o)
    wu, wd, wg, wp = bf(w_up), bf(w_down), bf(w_ple_gate), bf(w_ple)
    r = lax.broadcasted_iota(jnp.int32, (ATT_BLOCK, ATT_BLOCK), 0)
    c = lax.broadcasted_iota(jnp.int32, (ATT_BLOCK, ATT_BLOCK), 1)
    tri = (r > c).astype(BF16)

    xp = x_prompt.reshape(b * t, D_MODEL)
    kp, vp, q_bf, k_bf, v_bf, cpre, sga, sgc, conv_p = _proj_call(
        xp, g_mix[0], w_in_b, conv_w[0], t)
    att = _attn_prompt_call(q_bf, k_bf, v_bf, tri, b, t)
    yp = _post_call(xp, att, cpre, sga, sgc, p_prompt[0].reshape(b * t, PLE_DIM),
                    wa, wc, wo, g_ffn[0], wu, wd, g_ple[0], wg, wp, g_final, "post_prompt")

    xs = x_sample.reshape(db * dt, D_MODEL)
    buf = cache_conv[0]
    zeros = jnp.zeros((db, dt, CONV_WIDTH), F32)
    e1 = zeros.at[:, 0].set(buf[:, 1]).reshape(db * dt, CONV_WIDTH)
    e2 = zeros.at[:, 0].set(buf[:, 0]).at[:, 1].set(buf[:, 1]).reshape(db * dt, CONV_WIDTH)
    ks, vs, q_bf, k_bf, v_bf, cpre, sga, sgc, u_s = _proj_call(
        xs, g_mix[0], w_in_b, conv_w[0], dt, hist=(e1, e2))
    att = _attn_sample_call(q_bf, k_bf, v_bf, cache_k[0].reshape(db, past, SB_WIDTH),
                            cache_v[0].reshape(db, past, SB_WIDTH), tri, db, dt)
    ys = _post_call(xs, att, cpre, sga, sgc, p_sample[0].reshape(db * dt, PLE_DIM),
                    wa, wc, wo, g_ffn[0], wu, wd, g_ple[0], wg, wp, g_final, "post_sample")
    conv_s = u_s.reshape(db, dt, CONV_WIDTH)[:, dt - (CONV_K - 1):]

    hd = (N_HEADS, HEAD_DIM)
    return (yp.reshape(b, t, D_MODEL), ys.reshape(db, dt, D_MODEL),
            kp.reshape(1, b, t, *hd), vp.reshape(1, b, t, *hd), conv_p[None],
            ks.reshape(1, db, dt, *hd), vs.reshape(1, db, dt, *hd), conv_s[None])
```

```python
import functools
import math

import jax
import jax.numpy as jnp
from jax import lax
from jax.experimental import pallas as pl
from jax.experimental.pallas import tpu as pltpu

F32 = jnp.float32
BF16 = jnp.bfloat16

D_MODEL = 1024
N_HEADS = 8
HEAD_DIM = 64
SB_WIDTH = N_HEADS * HEAD_DIM
CONV_WIDTH = 512
CONV_K = 3
PLE_DIM = 256
D_FF = 4 * D_MODEL
EPS = 1e-6
N_PROJ = 3 * SB_WIDTH + 3 * CONV_WIDTH + 2 * D_MODEL

LANES = 128
HEADS_PER_TILE = LANES // HEAD_DIM
N_PAIRS = N_HEADS // HEADS_PER_TILE
ROW_TILE = 512
ATT_BLOCK = 256
FF_CHUNK = 1024
CONV_PAD = 8
LOG_ZERO = -104.0
VMEM_LIMIT = 52 * 1024 * 1024


def _resident(shape):
    nd = len(shape)
    return pl.BlockSpec(shape, lambda *_: (0,) * nd, pipeline_mode=pl.Buffered(1))


def _rms(x, g):
    return (x * lax.rsqrt(jnp.mean(x * x, axis=-1, keepdims=True) + EPS)) * g


def _dot(a, b):
    return jnp.dot(a, b, preferred_element_type=F32)


def _dot_nt(a, b):
    return lax.dot_general(a, b, (((1,), (1,)), ((), ())), preferred_element_type=F32)


def _proj_kernel(*refs, tiles_per_seq, seq_in_tile):
    prompt = seq_in_tile is None
    if prompt:
        (x_ref, g_ref, w_ref, cw_ref,
         k_out, v_out, q_bf, k_bf, v_bf, cpre, sga, sgc, cstate, s_ref) = refs
    else:
        (x_ref, g_ref, w_ref, cw_ref, e1_ref, e2_ref,
         k_out, v_out, q_bf, k_bf, v_bf, cpre, sga, sgc, cstate, s_ref) = refs
    tm = x_ref.shape[0]
    hb = _rms(x_ref[...], g_ref[...]).astype(BF16)

    def proj(lo, hi):
        return _dot(hb, w_ref[:, lo:hi])

    o = 0
    q_bf[...] = (proj(o, o + SB_WIDTH) * (1.0 / math.sqrt(HEAD_DIM))).astype(BF16)
    o += SB_WIDTH
    k = proj(o, o + SB_WIDTH)
    o += SB_WIDTH
    v = proj(o, o + SB_WIDTH)
    o += SB_WIDTH
    if prompt:
        kt = k.T
        k_out[0] = kt
        for j in range(tm // ATT_BLOCK):
            k_bf[0, j] = kt[:, j * ATT_BLOCK:(j + 1) * ATT_BLOCK].astype(BF16)
        v_out[0] = v.T
    else:
        k_out[...] = k
        k_bf[...] = k.astype(BF16)
        v_out[...] = v
    v_bf[...] = v.astype(BF16)
    cb = proj(o, o + CONV_WIDTH)
    o += CONV_WIDTH
    cc = proj(o, o + CONV_WIDTH)
    o += CONV_WIDTH
    cx = proj(o, o + CONV_WIDTH)
    o += CONV_WIDTH
    u = cc * cx

    if prompt:
        @pl.when(pl.program_id(0) % tiles_per_seq == 0)
        def _():
            s_ref[0:CONV_PAD, :] = jnp.zeros((CONV_PAD, CONV_WIDTH), F32)
    else:
        s_ref[0:CONV_PAD, :] = jnp.zeros((CONV_PAD, CONV_WIDTH), F32)
    s_ref[CONV_PAD:CONV_PAD + tm, :] = u
    prev1 = s_ref[CONV_PAD - 1:CONV_PAD - 1 + tm, :]
    prev2 = s_ref[CONV_PAD - 2:CONV_PAD - 2 + tm, :]
    if prompt:
        s_ref[CONV_PAD - 2:CONV_PAD, :] = u[tm - 2:tm, :]
        cstate[0] = u[tm - 2:tm, :]
    else:
        r = lax.broadcasted_iota(jnp.int32, (tm, 1), 0) % seq_in_tile
        prev1 = jnp.where(r < 1, e1_ref[...], prev1)
        prev2 = jnp.where(r < 2, e2_ref[...], prev2)
        cstate[...] = u
    conv = cw_ref[0:1, :] * prev2 + cw_ref[1:2, :] * prev1 + cw_ref[2:3, :] * u
    cpre[...] = (cb * conv).astype(BF16)

    sga[...] = jax.nn.sigmoid(proj(o, o + D_MODEL)).astype(BF16)
    o += D_MODEL
    sgc[...] = jax.nn.sigmoid(proj(o, o + D_MODEL)).astype(BF16)


def _proj_call(x, g, w_in, conv_w, seq_len, hist=None):
    n = x.shape[0]
    tm = min(ROW_TILE, n)
    grid = (n // tm,)
    sample = hist is not None
    sds = jax.ShapeDtypeStruct
    row = lambda w: pl.BlockSpec((tm, w), lambda t: (t, 0))
    in_specs = [row(D_MODEL), _resident((1, D_MODEL)), _resident((D_MODEL, N_PROJ)),
                _resident((CONV_K, CONV_WIDTH))]
    args = [x, g.reshape(1, D_MODEL), w_in, conv_w]
    if sample:
        in_specs += [row(CONV_WIDTH), row(CONV_WIDTH)]
        args += list(hist)
        tiles_per_seq, seq_in_tile = None, seq_len
        kv_shape = sds((n, SB_WIDTH), F32)
        kv_spec = row(SB_WIDTH)
        kbf_shape = sds((n, SB_WIDTH), BF16)
        kbf_spec = row(SB_WIDTH)
        cstate_shape = sds((n, CONV_WIDTH), F32)
        cstate_spec = row(CONV_WIDTH)
    else:
        tps = seq_len // tm
        nb = tm // ATT_BLOCK
        tiles_per_seq, seq_in_tile = tps, None
        batch = n // seq_len
        kv_shape = sds((batch, SB_WIDTH, seq_len), F32)
        kv_spec = pl.BlockSpec((1, SB_WIDTH, tm), lambda t: (t // tps, 0, t % tps))
        kbf_shape = sds((batch, seq_len // ATT_BLOCK, SB_WIDTH, ATT_BLOCK), BF16)
        kbf_spec = pl.BlockSpec((1, nb, SB_WIDTH, ATT_BLOCK), lambda t: (t // tps, t % tps, 0, 0))
        cstate_shape = sds((batch, CONV_K - 1, CONV_WIDTH), F32)
        cstate_spec = pl.BlockSpec((1, CONV_K - 1, CONV_WIDTH), lambda t: (t // tps, 0, 0))
    out_shape = (kv_shape, kv_shape,
                 sds((n, SB_WIDTH), BF16), kbf_shape, sds((n, SB_WIDTH), BF16),
                 sds((n, CONV_WIDTH), BF16), sds((n, D_MODEL), BF16), sds((n, D_MODEL), BF16),
                 cstate_shape)
    out_specs = (kv_spec, kv_spec, row(SB_WIDTH), kbf_spec, row(SB_WIDTH),
                 row(CONV_WIDTH), row(D_MODEL), row(D_MODEL), cstate_spec)
    return pl.pallas_call(
        functools.partial(_proj_kernel, tiles_per_seq=tiles_per_seq, seq_in_tile=seq_in_tile),
        grid=grid, in_specs=in_specs, out_specs=out_specs, out_shape=out_shape,
        scratch_shapes=[pltpu.VMEM((tm + CONV_PAD, CONV_WIDTH), F32)],
        compiler_params=pltpu.CompilerParams(
            dimension_semantics=("arbitrary",), vmem_limit_bytes=VMEM_LIMIT),
        name="proj_sample" if sample else "proj_prompt",
    )(*args)


def _sb_block(q2, kw, vw, tri, carry, bias, k_is_t, v_is_t):
    z = _dot(q2, kw) if k_is_t else _dot_nt(q2, kw)
    if bias is not None:
        z = z + bias
    sp = jnp.maximum(z, 0.0) + jnp.log(1.0 + jnp.exp(-jnp.abs(z)))
    hi = sp.astype(BF16)
    lo = (sp - hi.astype(F32)).astype(BF16)
    tail = _dot(hi, tri) + _dot(lo, tri)
    a = jnp.exp((z - sp) - tail + carry).astype(BF16)
    pv = _dot_nt(a, vw) if v_is_t else _dot(a, vw)
    return pv, carry - jnp.sum(sp, axis=1, keepdims=True)


def _stack_heads(qp):
    lane = lax.broadcasted_iota(jnp.int32, (1, LANES), 1)
    zero = jnp.zeros((), BF16)
    return jnp.concatenate([jnp.where(lane < HEAD_DIM, qp, zero),
                            jnp.where(lane >= HEAD_DIM, qp, zero)], axis=0)


def _unstack_heads(acc):
    lane = lax.broadcasted_iota(jnp.int32, (1, LANES), 1)
    half = acc.shape[0] // 2
    return jnp.where(lane < HEAD_DIM, acc[:half], acc[half:])


def _pair(p):
    return slice(p * LANES, (p + 1) * LANES)


def _attn_prompt_kernel(q_ref, kt_ref, v_ref, tri_ref, bias_ref, o_ref, carry_ref, acc_ref):
    i = pl.program_id(1)
    tq = q_ref.shape[0]

    def block(p, q2, j, carry, bias):
        k0 = pl.multiple_of(j * tq, tq)
        return _sb_block(q2, kt_ref[0, j, _pair(p), :], v_ref[pl.ds(k0, tq), _pair(p)],
                         tri_ref[...], carry, bias, True, False)

    @pl.when(i == 0)
    def _():
        for p in range(N_PAIRS):
            q2 = _stack_heads(q_ref[:, _pair(p)])
            acc_ref[p], _ = block(p, q2, i, jnp.zeros((2 * tq, 1), F32), bias_ref[...])

    @pl.when(i > 0)
    def _():
        mx = []
        for p in range(N_PAIRS):
            q2 = _stack_heads(q_ref[:, _pair(p)])
            pv0, carry = block(p, q2, i, jnp.zeros((2 * tq, 1), F32), bias_ref[...])
            pv1, carry = block(p, q2, i - 1, carry, None)
            acc_ref[p] = pv0 + pv1
            carry_ref[p] = carry
            mx.append(jnp.max(carry))
        for p in range(N_PAIRS):
            def body(state, p=p):
                j, _ = state
                q2 = _stack_heads(q_ref[:, _pair(p)])
                pv, carry = block(p, q2, j, carry_ref[p], None)
                acc_ref[p] += pv
                carry_ref[p] = carry
                return j - 1, jnp.max(carry)

            lax.while_loop(lambda s: (s[0] >= 0) & (s[1] > LOG_ZERO), body, (i - 2, mx[p]))

    for p in range(N_PAIRS):
        o_ref[:, _pair(p)] = _unstack_heads(acc_ref[p]).astype(BF16)


def _attn_prompt_call(q_bf, kt_bf, v_bf, tri, bias, batch, seq):
    tq = ATT_BLOCK
    nq = seq // tq
    return pl.pallas_call(
        _attn_prompt_kernel,
        grid=(batch, nq),
        in_specs=[pl.BlockSpec((tq, SB_WIDTH), lambda b, i: (b * nq + i, 0)),
                  pl.BlockSpec((1, nq, SB_WIDTH, tq), lambda b, i: (b, 0, 0, 0),
                               pipeline_mode=pl.Buffered(1)),
                  pl.BlockSpec((seq, SB_WIDTH), lambda b, i: (b, 0), pipeline_mode=pl.Buffered(1)),
                  _resident(tri.shape), _resident(bias.shape)],
        out_specs=pl.BlockSpec((tq, SB_WIDTH), lambda b, i: (b * nq + i, 0)),
        out_shape=jax.ShapeDtypeStruct((batch * seq, SB_WIDTH), BF16),
        scratch_shapes=[pltpu.VMEM((N_PAIRS, 2 * tq, 1), F32),
                        pltpu.VMEM((N_PAIRS, 2 * tq, LANES), F32)],
        compiler_params=pltpu.CompilerParams(
            dimension_semantics=("arbitrary", "arbitrary"), vmem_limit_bytes=VMEM_LIMIT),
        name="attn_prompt",
    )(q_bf, kt_bf, v_bf, tri, bias)


def _attn_sample_kernel(q_ref, kn_ref, vn_ref, ck0_ref, cv0_ref, ck_any, cv_any, tri_ref, bias_ref,
                        o_ref, carry_ref, acc_ref, kpad_ref, vpad_ref, kbuf_ref, vbuf_ref, sem):
    b = pl.program_id(0)
    tq = q_ref.shape[0]
    tk = tri_ref.shape[0]
    n_win = ck_any.shape[2] // tk
    kpad_ref[...] = jnp.zeros_like(kpad_ref)
    vpad_ref[...] = jnp.zeros_like(vpad_ref)
    kpad_ref[0:tq, :] = kn_ref[...]
    vpad_ref[0:tq, :] = vn_ref[...]
    mx = []
    for p in range(N_PAIRS):
        q2 = _stack_heads(q_ref[:, _pair(p)])
        pv0, carry = _sb_block(q2, kpad_ref[:, _pair(p)], vpad_ref[:, _pair(p)],
                               tri_ref[0:LANES, 0:LANES], jnp.zeros((2 * tq, 1), F32),
                               bias_ref[...], False, False)
        pv1, carry = _sb_block(q2, ck0_ref[0, _pair(p), :].astype(BF16),
                               cv0_ref[0, _pair(p), :].astype(BF16),
                               tri_ref[...], carry, None, True, True)
        acc_ref[p] = pv0 + pv1
        carry_ref[p] = carry
        mx.append(jnp.max(carry))
    for p in range(N_PAIRS):
        def window(src, dst, j, slot, p=p):
            k0 = pl.multiple_of(j * tk, tk)
            return pltpu.make_async_copy(src.at[b, _pair(p), pl.ds(k0, tk)], dst, sem.at[slot])

        def body(state, p=p, window=window):
            j, _ = state
            window(ck_any, kbuf_ref, j, 0).start()
            window(cv_any, vbuf_ref, j, 1).start()
            window(ck_any, kbuf_ref, j, 0).wait()
            window(cv_any, vbuf_ref, j, 1).wait()
            q2 = _stack_heads(q_ref[:, _pair(p)])
            pv, carry = _sb_block(q2, kbuf_ref[...].astype(BF16), vbuf_ref[...].astype(BF16),
                                  tri_ref[...], carry_ref[p], None, True, True)
            acc_ref[p] += pv
            carry_ref[p] = carry
            return j - 1, jnp.max(carry)

        lax.while_loop(lambda s: (s[0] >= 0) & (s[1] > LOG_ZERO), body, (n_win - 2, mx[p]))
    for p in range(N_PAIRS):
        o_ref[:, _pair(p)] = _unstack_heads(acc_ref[p]).astype(BF16)


def _attn_sample_call(q_bf, k_bf, v_bf, ckt, cvt, tri, bias, batch, seq):
    past = ckt.shape[2]
    tk = tri.shape[0]
    n_win = past // tk
    row = pl.BlockSpec((seq, SB_WIDTH), lambda b: (b, 0))
    last = pl.BlockSpec((1, SB_WIDTH, tk), lambda b: (b, 0, n_win - 1))
    hbm = pl.BlockSpec(memory_space=pl.ANY)
    return pl.pallas_call(
        _attn_sample_kernel,
        grid=(batch,),
        in_specs=[row, row, row, last, last, hbm, hbm, _resident(tri.shape), _resident(bias.shape)],
        out_specs=row,
        out_shape=jax.ShapeDtypeStruct((batch * seq, SB_WIDTH), BF16),
        scratch_shapes=[pltpu.VMEM((N_PAIRS, 2 * seq, 1), F32),
                        pltpu.VMEM((N_PAIRS, 2 * seq, LANES), F32),
                        pltpu.VMEM((LANES, SB_WIDTH), BF16), pltpu.VMEM((LANES, SB_WIDTH), BF16),
                        pltpu.VMEM((LANES, tk), F32), pltpu.VMEM((LANES, tk), F32),
                        pltpu.SemaphoreType.DMA((2,))],
        compiler_params=pltpu.CompilerParams(
            dimension_semantics=("arbitrary",), vmem_limit_bytes=VMEM_LIMIT),
        name="attn_sample",
    )(q_bf, k_bf, v_bf, ckt, cvt, ckt, cvt, tri, bias)


def _post_kernel(x_ref, att_ref, cpre_ref, sga_ref, sgc_ref, p_ref,
                 wa_ref, wc_ref, wo_ref, gf_ref, wu_ref, wd_ref, gp_ref, wg_ref, wp_ref, gl_ref,
                 y_ref):
    ya = _dot(att_ref[...], wa_ref[...])
    yc = _dot(cpre_ref[...], wc_ref[...])
    merged = sga_ref[...].astype(F32) * ya + sgc_ref[...].astype(F32) * yc
    x1 = x_ref[...] + _dot(merged.astype(BF16), wo_ref[...])
    h2 = _rms(x1, gf_ref[...]).astype(BF16)
    f = jnp.zeros_like(x1)
    for c in range(0, D_FF, FF_CHUNK):
        up = jnp.maximum(_dot(h2, wu_ref[:, c:c + FF_CHUNK]), 0.0)
        f = f + _dot((up * up).astype(BF16), wd_ref[c:c + FF_CHUNK, :])
    x2 = x1 + f
    h3 = _rms(x2, gp_ref[...]).astype(BF16)
    gate = jax.nn.sigmoid(_dot(h3, wg_ref[...]))
    x3 = x2 + gate * _dot(p_ref[...].astype(BF16), wp_ref[...])
    y_ref[...] = _rms(x3, gl_ref[...])


def _post_call(x, att, cpre, sga, sgc, p, wa, wc, wo, gf, wu, wd, gp, wg, wp, gl, name):
    n = x.shape[0]
    tm = min(ROW_TILE, n)
    row = lambda w: pl.BlockSpec((tm, w), lambda t: (t, 0))
    vec = lambda g: g.reshape(1, D_MODEL)
    return pl.pallas_call(
        _post_kernel,
        grid=(n // tm,),
        in_specs=[row(D_MODEL), row(SB_WIDTH), row(CONV_WIDTH), row(D_MODEL), row(D_MODEL),
                  row(PLE_DIM),
                  _resident(wa.shape), _resident(wc.shape), _resident(wo.shape),
                  _resident((1, D_MODEL)), _resident(wu.shape), _resident(wd.shape),
                  _resident((1, D_MODEL)), _resident(wg.shape), _resident(wp.shape),
                  _resident((1, D_MODEL))],
        out_specs=row(D_MODEL),
        out_shape=jax.ShapeDtypeStruct((n, D_MODEL), F32),
        compiler_params=pltpu.CompilerParams(
            dimension_semantics=("arbitrary",), vmem_limit_bytes=VMEM_LIMIT),
        name=name,
    )(x, att, cpre, sga, sgc, p, wa, wc, wo, vec(gf), wu, wd, vec(gp), wg, wp, vec(gl))


def _causal_bias(rows, cols, period):
    r = lax.broadcasted_iota(jnp.int32, (rows, cols), 0) % period
    c = lax.broadcasted_iota(jnp.int32, (rows, cols), 1)
    return jnp.where(c < r, 0.0, -jnp.inf).astype(F32)


def kernel(x_prompt, x_sample, p_prompt, p_sample, cache_k, cache_v, cache_conv,
           g_mix, w_in, conv_w, w_attn_out, w_conv_out, w_o,
           g_ffn, w_up, w_down, g_ple, w_ple_gate, w_ple, g_final):
    depth = w_in.shape[0]
    assert depth == 1
    b, t, _ = x_prompt.shape
    db, dt, _ = x_sample.shape
    past = cache_k.shape[2]
    bf = lambda w: w[0].astype(BF16)
    w_in_b, wa, wc, wo = bf(w_in), bf(w_attn_out), bf(w_conv_out), bf(w_o)
    wu, wd, wg, wp = bf(w_up), bf(w_down), bf(w_ple_gate), bf(w_ple)
    r = lax.broadcasted_iota(jnp.int32, (ATT_BLOCK, ATT_BLOCK), 0)
    c = lax.broadcasted_iota(jnp.int32, (ATT_BLOCK, ATT_BLOCK), 1)
    tri = (r > c).astype(BF16)
    hd = (N_HEADS, HEAD_DIM)

    xp = x_prompt.reshape(b * t, D_MODEL)
    kpt, vpt, q_bf, kt_bf, v_bf, cpre, sga, sgc, conv_p = _proj_call(
        xp, g_mix[0], w_in_b, conv_w[0], t)
    att = _attn_prompt_call(q_bf, kt_bf, v_bf, tri,
                            _causal_bias(2 * ATT_BLOCK, ATT_BLOCK, ATT_BLOCK), b, t)
    yp = _post_call(xp, att, cpre, sga, sgc, p_prompt[0].reshape(b * t, PLE_DIM),
                    wa, wc, wo, g_ffn[0], wu, wd, g_ple[0], wg, wp, g_final, "post_prompt")
    kp = kpt.reshape(b, *hd, t).transpose(0, 3, 1, 2)
    vp = vpt.reshape(b, *hd, t).transpose(0, 3, 1, 2)

    xs = x_sample.reshape(db * dt, D_MODEL)
    buf = cache_conv[0]
    zeros = jnp.zeros((db, dt, CONV_WIDTH), F32)
    e1 = zeros.at[:, 0].set(buf[:, 1]).reshape(db * dt, CONV_WIDTH)
    e2 = zeros.at[:, 0].set(buf[:, 0]).at[:, 1].set(buf[:, 1]).reshape(db * dt, CONV_WIDTH)
    ks, vs, q_bf, k_bf, v_bf, cpre, sga, sgc, u_s = _proj_call(
        xs, g_mix[0], w_in_b, conv_w[0], dt, hist=(e1, e2))
    ckt = cache_k[0].transpose(0, 2, 3, 1).reshape(db, SB_WIDTH, past)
    cvt = cache_v[0].transpose(0, 2, 3, 1).reshape(db, SB_WIDTH, past)
    att = _attn_sample_call(q_bf, k_bf, v_bf, ckt, cvt, tri, _causal_bias(2 * dt, LANES, dt), db, dt)
    ys = _post_call(xs, att, cpre, sga, sgc, p_sample[0].reshape(db * dt, PLE_DIM),
                    wa, wc, wo, g_ffn[0], wu, wd, g_ple[0], wg, wp, g_final, "post_sample")
    conv_s = u_s.reshape(db, dt, CONV_WIDTH)[:, dt - (CONV_K - 1):]

    return (yp.reshape(b, t, D_MODEL), ys.reshape(db, dt, D_MODEL),
            kp[None], vp[None], conv_p[None],
            ks.reshape(1, db, dt, *hd), vs.reshape(1, db, dt, *hd), conv_s[None])
```

```python
import functools
import math

import jax
import jax.numpy as jnp
from jax import lax
from jax.experimental import pallas as pl
from jax.experimental.pallas import tpu as pltpu

F32 = jnp.float32
BF16 = jnp.bfloat16

D_MODEL = 1024
N_HEADS = 8
HEAD_DIM = 64
SB_WIDTH = N_HEADS * HEAD_DIM
CONV_WIDTH = 512
CONV_K = 3
PLE_DIM = 256
D_FF = 4 * D_MODEL
EPS = 1e-6
N_PROJ = 3 * SB_WIDTH + 3 * CONV_WIDTH + 2 * D_MODEL

LANES = 128
HEADS_PER_TILE = LANES // HEAD_DIM
N_PAIRS = N_HEADS // HEADS_PER_TILE
ROW_TILE = 512
ATT_BLOCK = 256
ATT_CHUNK = 512
FF_CHUNK = 1024
CONV_PAD = 8
LOG2E = 1.4426950408889634
LOG2_ZERO = -150.0
VMEM_LIMIT = 52 * 1024 * 1024


def _resident(shape):
    nd = len(shape)
    return pl.BlockSpec(shape, lambda *_: (0,) * nd, pipeline_mode=pl.Buffered(1))


def _rms(x, g):
    return (x * lax.rsqrt(jnp.mean(x * x, axis=-1, keepdims=True) + EPS)) * g


def _dot(a, b):
    return jnp.dot(a, b, preferred_element_type=F32)


def _dot_nt(a, b):
    return lax.dot_general(a, b, (((1,), (1,)), ((), ())), preferred_element_type=F32)


def _proj_kernel(*refs, tiles_per_seq, seq_in_tile):
    prompt = seq_in_tile is None
    if prompt:
        (x_ref, g_ref, w_ref, cw_ref,
         k_out, v_out, q_bf, k_bf, v_bf, cpre, sga, sgc, cstate, s_ref) = refs
    else:
        (x_ref, g_ref, w_ref, cw_ref, e1_ref, e2_ref,
         k_out, v_out, q_bf, k_bf, v_bf, cpre, sga, sgc, cstate, s_ref) = refs
    tm = x_ref.shape[0]
    hb = _rms(x_ref[...], g_ref[...]).astype(BF16)
    o_q, o_k, o_v = 0, SB_WIDTH, 2 * SB_WIDTH
    o_cb = 3 * SB_WIDTH
    o_cc, o_cx = o_cb + CONV_WIDTH, o_cb + 2 * CONV_WIDTH
    o_ga = o_cb + 3 * CONV_WIDTH
    o_gc = o_ga + D_MODEL

    def proj(lo, width):
        return _dot(hb, w_ref[:, lo:lo + width])

    sga[...] = jax.nn.sigmoid(proj(o_ga, D_MODEL)).astype(BF16)
    sgc[...] = jax.nn.sigmoid(proj(o_gc, D_MODEL)).astype(BF16)

    cb = proj(o_cb, CONV_WIDTH)
    u = proj(o_cc, CONV_WIDTH) * proj(o_cx, CONV_WIDTH)
    if prompt:
        @pl.when(pl.program_id(0) % tiles_per_seq == 0)
        def _():
            s_ref[0:CONV_PAD, :] = jnp.zeros((CONV_PAD, CONV_WIDTH), F32)
    else:
        s_ref[0:CONV_PAD, :] = jnp.zeros((CONV_PAD, CONV_WIDTH), F32)
    s_ref[CONV_PAD:CONV_PAD + tm, :] = u
    prev1 = s_ref[CONV_PAD - 1:CONV_PAD - 1 + tm, :]
    prev2 = s_ref[CONV_PAD - 2:CONV_PAD - 2 + tm, :]
    if prompt:
        s_ref[CONV_PAD - 2:CONV_PAD, :] = u[tm - 2:tm, :]
        cstate[0] = u[tm - 2:tm, :]
    else:
        r = lax.broadcasted_iota(jnp.int32, (tm, 1), 0) % seq_in_tile
        prev1 = jnp.where(r < 1, e1_ref[...], prev1)
        prev2 = jnp.where(r < 2, e2_ref[...], prev2)
        cstate[...] = u
    conv = cw_ref[0:1, :] * prev2 + cw_ref[1:2, :] * prev1 + cw_ref[2:3, :] * u
    cpre[...] = (cb * conv).astype(BF16)

    k = proj(o_k, SB_WIDTH)
    v = proj(o_v, SB_WIDTH)
    if prompt:
        kt = k.T
        k_out[0] = kt
        for j in range(tm // ATT_BLOCK):
            k_bf[0, j] = kt[:, j * ATT_BLOCK:(j + 1) * ATT_BLOCK].astype(BF16)
        v_out[0] = v.T
    else:
        k_out[...] = k
        k_bf[...] = k.astype(BF16)
        v_out[...] = v
    v_bf[...] = v.astype(BF16)
    q_bf[...] = (proj(o_q, SB_WIDTH) * (LOG2E / math.sqrt(HEAD_DIM))).astype(BF16)


def _proj_call(x, g, w_in, conv_w, seq_len, hist=None):
    n = x.shape[0]
    tm = min(ROW_TILE, n)
    grid = (n // tm,)
    sample = hist is not None
    sds = jax.ShapeDtypeStruct
    row = lambda w: pl.BlockSpec((tm, w), lambda t: (t, 0))
    in_specs = [row(D_MODEL), _resident((1, D_MODEL)), _resident((D_MODEL, N_PROJ)),
                _resident((CONV_K, CONV_WIDTH))]
    args = [x, g.reshape(1, D_MODEL), w_in, conv_w]
    if sample:
        in_specs += [row(CONV_WIDTH), row(CONV_WIDTH)]
        args += list(hist)
        tiles_per_seq, seq_in_tile = None, seq_len
        kv_shape = sds((n, SB_WIDTH), F32)
        kv_spec = row(SB_WIDTH)
        kbf_shape = sds((n, SB_WIDTH), BF16)
        kbf_spec = row(SB_WIDTH)
        cstate_shape = sds((n, CONV_WIDTH), F32)
        cstate_spec = row(CONV_WIDTH)
    else:
        tps = seq_len // tm
        nb = tm // ATT_BLOCK
        tiles_per_seq, seq_in_tile = tps, None
        batch = n // seq_len
        kv_shape = sds((batch, SB_WIDTH, seq_len), F32)
        kv_spec = pl.BlockSpec((1, SB_WIDTH, tm), lambda t: (t // tps, 0, t % tps))
        kbf_shape = sds((batch, seq_len // ATT_BLOCK, SB_WIDTH, ATT_BLOCK), BF16)
        kbf_spec = pl.BlockSpec((1, nb, SB_WIDTH, ATT_BLOCK), lambda t: (t // tps, t % tps, 0, 0))
        cstate_shape = sds((batch, CONV_K - 1, CONV_WIDTH), F32)
        cstate_spec = pl.BlockSpec((1, CONV_K - 1, CONV_WIDTH), lambda t: (t // tps, 0, 0))
    out_shape = (kv_shape, kv_shape,
                 sds((n, SB_WIDTH), BF16), kbf_shape, sds((n, SB_WIDTH), BF16),
                 sds((n, CONV_WIDTH), BF16), sds((n, D_MODEL), BF16), sds((n, D_MODEL), BF16),
                 cstate_shape)
    out_specs = (kv_spec, kv_spec, row(SB_WIDTH), kbf_spec, row(SB_WIDTH),
                 row(CONV_WIDTH), row(D_MODEL), row(D_MODEL), cstate_spec)
    return pl.pallas_call(
        functools.partial(_proj_kernel, tiles_per_seq=tiles_per_seq, seq_in_tile=seq_in_tile),
        grid=grid, in_specs=in_specs, out_specs=out_specs, out_shape=out_shape,
        scratch_shapes=[pltpu.VMEM((tm + CONV_PAD, CONV_WIDTH), F32)],
        compiler_params=pltpu.CompilerParams(
            dimension_semantics=("arbitrary",), vmem_limit_bytes=VMEM_LIMIT),
        name="proj_sample" if sample else "proj_prompt",
    )(*args)


def _sb_block(q2, kw, vw, tri, carry, bias, k_is_t, v_is_t):
    z = _dot(q2, kw) if k_is_t else _dot_nt(q2, kw)
    if bias is not None:
        z = z + bias
    sp = jnp.maximum(z, 0.0) + jnp.log(1.0 + jnp.exp2(-jnp.abs(z))) * LOG2E
    tail = _dot(sp.astype(BF16), tri)
    a = jnp.exp2(z - tail + carry).astype(BF16)
    pv = _dot_nt(a, vw) if v_is_t else _dot(a, vw)
    return pv, carry - tail[:, 0:1]


def _stack_heads(qp):
    lane = lax.broadcasted_iota(jnp.int32, (1, LANES), 1)
    zero = jnp.zeros((), BF16)
    return jnp.concatenate([jnp.where(lane < HEAD_DIM, qp, zero),
                            jnp.where(lane >= HEAD_DIM, qp, zero)], axis=0)


def _unstack_heads(acc):
    lane = lax.broadcasted_iota(jnp.int32, (1, LANES), 1)
    half = acc.shape[0] // 2
    return jnp.where(lane < HEAD_DIM, acc[:half], acc[half:])


def _pair(p):
    return slice(p * LANES, (p + 1) * LANES)


def _sweep(q2, blocks, acc_ref, carry_ref, first):
    rows = q2.shape[0]
    ch = min(ATT_CHUNK, rows)
    mxv = None
    for r0 in range(0, rows, ch):
        rs = slice(r0, r0 + ch)
        carry = jnp.zeros((ch, 1), F32) if first else carry_ref[rs]
        pv = None
        for kw, vw, tri, bias_ref, k_is_t, v_is_t in blocks:
            bias = None if bias_ref is None else bias_ref[rs]
            pv_b, carry = _sb_block(q2[rs], kw(), vw(), tri(), carry, bias, k_is_t, v_is_t)
            pv = pv_b if pv is None else pv + pv_b
        acc_ref[rs] = pv if first else acc_ref[rs] + pv
        carry_ref[rs] = carry
        mxv = carry if mxv is None else jnp.maximum(mxv, carry)
    return jnp.max(mxv)


def _attn_prompt_kernel(q_ref, kt_ref, v_ref, tri_ref, bias_ref, o_ref, carry_ref, acc_ref):
    i = pl.program_id(1)
    tq = q_ref.shape[0]

    def block(p, j, bias):
        k0 = pl.multiple_of(j * tq, tq)
        return (lambda: kt_ref[0, j, _pair(p), :], lambda: v_ref[pl.ds(k0, tq), _pair(p)],
                lambda: tri_ref[...], bias, True, False)

    @pl.when(i == 0)
    def _():
        for p in range(N_PAIRS):
            q2 = _stack_heads(q_ref[:, _pair(p)])
            _sweep(q2, [block(p, i, bias_ref)], acc_ref.at[p], carry_ref.at[p], True)

    @pl.when(i > 0)
    def _():
        mx = []
        for p in range(N_PAIRS):
            q2 = _stack_heads(q_ref[:, _pair(p)])
            mx.append(_sweep(q2, [block(p, i, bias_ref), block(p, i - 1, None)],
                             acc_ref.at[p], carry_ref.at[p], True))
        for p in range(N_PAIRS):
            def body(state, p=p):
                j, _ = state
                q2 = _stack_heads(q_ref[:, _pair(p)])
                return j - 1, _sweep(q2, [block(p, j, None)], acc_ref.at[p], carry_ref.at[p], False)

            lax.while_loop(lambda s: (s[0] >= 0) & (s[1] > LOG2_ZERO), body, (i - 2, mx[p]))

    for p in range(N_PAIRS):
        o_ref[:, _pair(p)] = _unstack_heads(acc_ref[p]).astype(BF16)


def _attn_prompt_call(q_bf, kt_bf, v_bf, tri, bias, batch, seq):
    tq = ATT_BLOCK
    nq = seq // tq
    return pl.pallas_call(
        _attn_prompt_kernel,
        grid=(batch, nq),
        in_specs=[pl.BlockSpec((tq, SB_WIDTH), lambda b, i: (b * nq + i, 0)),
                  pl.BlockSpec((1, nq, SB_WIDTH, tq), lambda b, i: (b, 0, 0, 0),
                               pipeline_mode=pl.Buffered(1)),
                  pl.BlockSpec((seq, SB_WIDTH), lambda b, i: (b, 0), pipeline_mode=pl.Buffered(1)),
                  _resident(tri.shape), _resident(bias.shape)],
        out_specs=pl.BlockSpec((tq, SB_WIDTH), lambda b, i: (b * nq + i, 0)),
        out_shape=jax.ShapeDtypeStruct((batch * seq, SB_WIDTH), BF16),
        scratch_shapes=[pltpu.VMEM((N_PAIRS, 2 * tq, 1), F32),
                        pltpu.VMEM((N_PAIRS, 2 * tq, LANES), F32)],
        compiler_params=pltpu.CompilerParams(
            dimension_semantics=("arbitrary", "arbitrary"), vmem_limit_bytes=VMEM_LIMIT),
        name="attn_prompt",
    )(q_bf, kt_bf, v_bf, tri, bias)


def _attn_sample_kernel(q_ref, kn_ref, vn_ref, ck0_ref, cv0_ref, ck_any, cv_any, tri_ref, bias_ref,
                        o_ref, carry_ref, acc_ref, kpad_ref, vpad_ref, kbuf_ref, vbuf_ref, sem):
    b = pl.program_id(0)
    tq = q_ref.shape[0]
    tk = tri_ref.shape[0]
    n_win = ck_any.shape[2] // tk
    kpad_ref[...] = jnp.zeros_like(kpad_ref)
    vpad_ref[...] = jnp.zeros_like(vpad_ref)
    kpad_ref[0:tq, :] = kn_ref[...]
    vpad_ref[0:tq, :] = vn_ref[...]
    mx = []
    for p in range(N_PAIRS):
        q2 = _stack_heads(q_ref[:, _pair(p)])
        new = (lambda p=p: kpad_ref[:, _pair(p)], lambda p=p: vpad_ref[:, _pair(p)],
               lambda: tri_ref[0:LANES, 0:LANES], bias_ref, False, False)
        last = (lambda p=p: ck0_ref[0, _pair(p), :].astype(BF16),
                lambda p=p: cv0_ref[0, _pair(p), :].astype(BF16),
                lambda: tri_ref[...], None, True, True)
        mx.append(_sweep(q2, [new, last], acc_ref.at[p], carry_ref.at[p], True))
    for p in range(N_PAIRS):
        def window(src, dst, j, slot, p=p):
            k0 = pl.multiple_of(j * tk, tk)
            return pltpu.make_async_copy(src.at[b, _pair(p), pl.ds(k0, tk)], dst, sem.at[slot])

        def body(state, p=p, window=window):
            j, _ = state
            window(ck_any, kbuf_ref, j, 0).start()
            window(cv_any, vbuf_ref, j, 1).start()
            window(ck_any, kbuf_ref, j, 0).wait()
            window(cv_any, vbuf_ref, j, 1).wait()
            q2 = _stack_heads(q_ref[:, _pair(p)])
            win = (lambda: kbuf_ref[...].astype(BF16), lambda: vbuf_ref[...].astype(BF16),
                   lambda: tri_ref[...], None, True, True)
            return j - 1, _sweep(q2, [win], acc_ref.at[p], carry_ref.at[p], False)

        lax.while_loop(lambda s: (s[0] >= 0) & (s[1] > LOG2_ZERO), body, (n_win - 2, mx[p]))
    for p in range(N_PAIRS):
        o_ref[:, _pair(p)] = _unstack_heads(acc_ref[p]).astype(BF16)


def _attn_sample_call(q_bf, k_bf, v_bf, ckt, cvt, tri, bias, batch, seq):
    past = ckt.shape[2]
    tk = tri.shape[0]
    n_win = past // tk
    row = pl.BlockSpec((seq, SB_WIDTH), lambda b: (b, 0))
    last = pl.BlockSpec((1, SB_WIDTH, tk), lambda b: (b, 0, n_win - 1))
    hbm = pl.BlockSpec(memory_space=pl.ANY)
    return pl.pallas_call(
        _attn_sample_kernel,
        grid=(batch,),
        in_specs=[row, row, row, last, last, hbm, hbm, _resident(tri.shape), _resident(bias.shape)],
        out_specs=row,
        out_shape=jax.ShapeDtypeStruct((batch * seq, SB_WIDTH), BF16),
        scratch_shapes=[pltpu.VMEM((N_PAIRS, 2 * seq, 1), F32),
                        pltpu.VMEM((N_PAIRS, 2 * seq, LANES), F32),
                        pltpu.VMEM((LANES, SB_WIDTH), BF16), pltpu.VMEM((LANES, SB_WIDTH), BF16),
                        pltpu.VMEM((LANES, tk), F32), pltpu.VMEM((LANES, tk), F32),
                        pltpu.SemaphoreType.DMA((2,))],
        compiler_params=pltpu.CompilerParams(
            dimension_semantics=("arbitrary",), vmem_limit_bytes=VMEM_LIMIT),
        name="attn_sample",
    )(q_bf, k_bf, v_bf, ckt, cvt, ckt, cvt, tri, bias)


def _post_kernel(x_ref, att_ref, cpre_ref, sga_ref, sgc_ref, p_ref,
                 wa_ref, wc_ref, wo_ref, gf_ref, wu_ref, wd_ref, gp_ref, wg_ref, wp_ref, gl_ref,
                 y_ref):
    ya = _dot(att_ref[...], wa_ref[...])
    yc = _dot(cpre_ref[...], wc_ref[...])
    merged = sga_ref[...].astype(F32) * ya + sgc_ref[...].astype(F32) * yc
    x1 = x_ref[...] + _dot(merged.astype(BF16), wo_ref[...])
    h2 = _rms(x1, gf_ref[...]).astype(BF16)
    f = jnp.zeros_like(x1)
    for c in range(0, D_FF, FF_CHUNK):
        up = jnp.maximum(_dot(h2, wu_ref[:, c:c + FF_CHUNK]), 0.0)
        f = f + _dot((up * up).astype(BF16), wd_ref[c:c + FF_CHUNK, :])
    x2 = x1 + f
    h3 = _rms(x2, gp_ref[...]).astype(BF16)
    gate = jax.nn.sigmoid(_dot(h3, wg_ref[...]))
    x3 = x2 + gate * _dot(p_ref[...].astype(BF16), wp_ref[...])
    y_ref[...] = _rms(x3, gl_ref[...])


def _post_call(x, att, cpre, sga, sgc, p, wa, wc, wo, gf, wu, wd, gp, wg, wp, gl, name):
    n = x.shape[0]
    tm = min(ROW_TILE, n)
    row = lambda w: pl.BlockSpec((tm, w), lambda t: (t, 0))
    vec = lambda g: g.reshape(1, D_MODEL)
    return pl.pallas_call(
        _post_kernel,
        grid=(n // tm,),
        in_specs=[row(D_MODEL), row(SB_WIDTH), row(CONV_WIDTH), row(D_MODEL), row(D_MODEL),
                  row(PLE_DIM),
                  _resident(wa.shape), _resident(wc.shape), _resident(wo.shape),
                  _resident((1, D_MODEL)), _resident(wu.shape), _resident(wd.shape),
                  _resident((1, D_MODEL)), _resident(wg.shape), _resident(wp.shape),
                  _resident((1, D_MODEL))],
        out_specs=row(D_MODEL),
        out_shape=jax.ShapeDtypeStruct((n, D_MODEL), F32),
        compiler_params=pltpu.CompilerParams(
            dimension_semantics=("arbitrary",), vmem_limit_bytes=VMEM_LIMIT),
        name=name,
    )(x, att, cpre, sga, sgc, p, wa, wc, wo, vec(gf), wu, wd, vec(gp), wg, wp, vec(gl))


def _causal_bias(rows, cols, period):
    r = lax.broadcasted_iota(jnp.int32, (rows, cols), 0) % period
    c = lax.broadcasted_iota(jnp.int32, (rows, cols), 1)
    return jnp.where(c < r, 0.0, -jnp.inf).astype(F32)


def kernel(x_prompt, x_sample, p_prompt, p_sample, cache_k, cache_v, cache_conv,
           g_mix, w_in, conv_w, w_attn_out, w_conv_out, w_o,
           g_ffn, w_up, w_down, g_ple, w_ple_gate, w_ple, g_final):
    depth = w_in.shape[0]
    assert depth == 1
    b, t, _ = x_prompt.shape
    db, dt, _ = x_sample.shape
    past = cache_k.shape[2]
    bf = lambda w: w[0].astype(BF16)
    w_in_b, wa, wc, wo = bf(w_in), bf(w_attn_out), bf(w_conv_out), bf(w_o)
    wu, wd, wg, wp = bf(w_up), bf(w_down), bf(w_ple_gate), bf(w_ple)
    r = lax.broadcasted_iota(jnp.int32, (ATT_BLOCK, ATT_BLOCK), 0)
    c = lax.broadcasted_iota(jnp.int32, (ATT_BLOCK, ATT_BLOCK), 1)
    tri = (r >= c).astype(BF16)
    hd = (N_HEADS, HEAD_DIM)

    xp = x_prompt.reshape(b * t, D_MODEL)
    kpt, vpt, q_bf, kt_bf, v_bf, cpre, sga, sgc, conv_p = _proj_call(
        xp, g_mix[0], w_in_b, conv_w[0], t)
    att = _attn_prompt_call(q_bf, kt_bf, v_bf, tri,
                            _causal_bias(2 * ATT_BLOCK, ATT_BLOCK, ATT_BLOCK), b, t)
    yp = _post_call(xp, att, cpre, sga, sgc, p_prompt[0].reshape(b * t, PLE_DIM),
                    wa, wc, wo, g_ffn[0], wu, wd, g_ple[0], wg, wp, g_final, "post_prompt")
    kp = kpt.reshape(b, *hd, t).transpose(0, 3, 1, 2)
    vp = vpt.reshape(b, *hd, t).transpose(0, 3, 1, 2)

    xs = x_sample.reshape(db * dt, D_MODEL)
    buf = cache_conv[0]
    zeros = jnp.zeros((db, dt, CONV_WIDTH), F32)
    e1 = zeros.at[:, 0].set(buf[:, 1]).reshape(db * dt, CONV_WIDTH)
    e2 = zeros.at[:, 0].set(buf[:, 0]).at[:, 1].set(buf[:, 1]).reshape(db * dt, CONV_WIDTH)
    ks, vs, q_bf, k_bf, v_bf, cpre, sga, sgc, u_s = _proj_call(
        xs, g_mix[0], w_in_b, conv_w[0], dt, hist=(e1, e2))
    ckt = cache_k[0].transpose(0, 2, 3, 1).reshape(db, SB_WIDTH, past)
    cvt = cache_v[0].transpose(0, 2, 3, 1).reshape(db, SB_WIDTH, past)
    att = _attn_sample_call(q_bf, k_bf, v_bf, ckt, cvt, tri, _causal_bias(2 * dt, LANES, dt), db, dt)
    ys = _post_call(xs, att, cpre, sga, sgc, p_sample[0].reshape(db * dt, PLE_DIM),
                    wa, wc, wo, g_ffn[0], wu, wd, g_ple[0], wg, wp, g_final, "post_sample")
    conv_s = u_s.reshape(db, dt, CONV_WIDTH)[:, dt - (CONV_K - 1):]

    return (yp.reshape(b, t, D_MODEL), ys.reshape(db, dt, D_MODEL),
            kp[None], vp[None], conv_p[None],
            ks.reshape(1, db, dt, *hd), vs.reshape(1, db, dt, *hd), conv_s[None])
```

```python
import functools
import math

import jax
import jax.numpy as jnp
from jax import lax
from jax.experimental import pallas as pl
from jax.experimental.pallas import tpu as pltpu

F32 = jnp.float32
BF16 = jnp.bfloat16

D_MODEL = 1024
N_HEADS = 8
HEAD_DIM = 64
SB_WIDTH = N_HEADS * HEAD_DIM
CONV_WIDTH = 512
CONV_K = 3
PLE_DIM = 256
D_FF = 4 * D_MODEL
EPS = 1e-6
N_PROJ = 3 * SB_WIDTH + 3 * CONV_WIDTH + 2 * D_MODEL

LANES = 128
HEADS_PER_TILE = LANES // HEAD_DIM
N_PAIRS = N_HEADS // HEADS_PER_TILE
ROW_TILE = 512
ATT_BLOCK = 256
ATT_CHUNK = 512
FF_CHUNK = 1024
POST_GROUPS = 2
CONV_PAD = 8
LOG2E = 1.4426950408889634
LOG2_ZERO = -150.0
VMEM_LIMIT = 52 * 1024 * 1024


def _resident(shape):
    nd = len(shape)
    return pl.BlockSpec(shape, lambda *_: (0,) * nd, pipeline_mode=pl.Buffered(1))


def _rms(x, g):
    return (x * lax.rsqrt(jnp.mean(x * x, axis=-1, keepdims=True) + EPS)) * g


def _dot(a, b):
    return jnp.dot(a, b, preferred_element_type=F32)


def _dot_nt(a, b):
    return lax.dot_general(a, b, (((1,), (1,)), ((), ())), preferred_element_type=F32)


def _proj_kernel(*refs, tiles_per_seq, seq_in_tile):
    prompt = seq_in_tile is None
    if prompt:
        (x_ref, g_ref, w_ref, cw_ref, wc_ref,
         k_out, v_out, q_bf, k_bf, v_bf, sga, gyc, cstate, s_ref) = refs
    else:
        (x_ref, g_ref, w_ref, cw_ref, wc_ref, e1_ref, e2_ref,
         k_out, v_out, q_bf, k_bf, v_bf, sga, gyc, cstate, s_ref) = refs
    tm = x_ref.shape[0]
    hb = _rms(x_ref[...], g_ref[...]).astype(BF16)
    o_q, o_k, o_v = 0, SB_WIDTH, 2 * SB_WIDTH
    o_cb = 3 * SB_WIDTH
    o_cc, o_cx = o_cb + CONV_WIDTH, o_cb + 2 * CONV_WIDTH
    o_ga = o_cb + 3 * CONV_WIDTH
    o_gc = o_ga + D_MODEL

    def proj(lo, width):
        return _dot(hb, w_ref[:, lo:lo + width])

    cb = proj(o_cb, CONV_WIDTH)
    u = proj(o_cc, CONV_WIDTH) * proj(o_cx, CONV_WIDTH)
    if prompt:
        @pl.when(pl.program_id(0) % tiles_per_seq == 0)
        def _():
            s_ref[0:CONV_PAD, :] = jnp.zeros((CONV_PAD, CONV_WIDTH), F32)
    else:
        s_ref[0:CONV_PAD, :] = jnp.zeros((CONV_PAD, CONV_WIDTH), F32)
    s_ref[CONV_PAD:CONV_PAD + tm, :] = u
    prev1 = s_ref[CONV_PAD - 1:CONV_PAD - 1 + tm, :]
    prev2 = s_ref[CONV_PAD - 2:CONV_PAD - 2 + tm, :]
    if prompt:
        s_ref[CONV_PAD - 2:CONV_PAD, :] = u[tm - 2:tm, :]
        cstate[0] = u[tm - 2:tm, :]
    else:
        r = lax.broadcasted_iota(jnp.int32, (tm, 1), 0) % seq_in_tile
        prev1 = jnp.where(r < 1, e1_ref[...], prev1)
        prev2 = jnp.where(r < 2, e2_ref[...], prev2)
        cstate[...] = u
    conv = cw_ref[0:1, :] * prev2 + cw_ref[1:2, :] * prev1 + cw_ref[2:3, :] * u
    cpre = (cb * conv).astype(BF16)

    sga[...] = jax.nn.sigmoid(proj(o_ga, D_MODEL)).astype(BF16)
    gyc[...] = (jax.nn.sigmoid(proj(o_gc, D_MODEL)) * _dot(cpre, wc_ref[...])).astype(BF16)

    k = proj(o_k, SB_WIDTH)
    v = proj(o_v, SB_WIDTH)
    if prompt:
        kt = k.T
        k_out[0] = kt
        for j in range(tm // ATT_BLOCK):
            k_bf[0, j] = kt[:, j * ATT_BLOCK:(j + 1) * ATT_BLOCK].astype(BF16)
        v_out[0] = v.T
    else:
        k_out[...] = k
        k_bf[...] = k.astype(BF16)
        v_out[...] = v
    v_bf[...] = v.astype(BF16)
    q_bf[...] = (proj(o_q, SB_WIDTH) * (LOG2E / math.sqrt(HEAD_DIM))).astype(BF16)


def _proj_call(x, g, w_in, conv_w, w_conv_out, seq_len, hist=None):
    n = x.shape[0]
    tm = min(ROW_TILE, n)
    grid = (n // tm,)
    sample = hist is not None
    sds = jax.ShapeDtypeStruct
    row = lambda w: pl.BlockSpec((tm, w), lambda t: (t, 0))
    in_specs = [row(D_MODEL), _resident((1, D_MODEL)), _resident((D_MODEL, N_PROJ)),
                _resident((CONV_K, CONV_WIDTH)), _resident((CONV_WIDTH, D_MODEL))]
    args = [x, g.reshape(1, D_MODEL), w_in, conv_w, w_conv_out]
    if sample:
        in_specs += [row(CONV_WIDTH), row(CONV_WIDTH)]
        args += list(hist)
        tiles_per_seq, seq_in_tile = None, seq_len
        kv_shape = sds((n, SB_WIDTH), F32)
        kv_spec = row(SB_WIDTH)
        kbf_shape = sds((n, SB_WIDTH), BF16)
        kbf_spec = row(SB_WIDTH)
        cstate_shape = sds((n, CONV_WIDTH), F32)
        cstate_spec = row(CONV_WIDTH)
    else:
        tps = seq_len // tm
        nb = tm // ATT_BLOCK
        tiles_per_seq, seq_in_tile = tps, None
        batch = n // seq_len
        kv_shape = sds((batch, SB_WIDTH, seq_len), F32)
        kv_spec = pl.BlockSpec((1, SB_WIDTH, tm), lambda t: (t // tps, 0, t % tps))
        kbf_shape = sds((batch, seq_len // ATT_BLOCK, SB_WIDTH, ATT_BLOCK), BF16)
        kbf_spec = pl.BlockSpec((1, nb, SB_WIDTH, ATT_BLOCK), lambda t: (t // tps, t % tps, 0, 0))
        cstate_shape = sds((batch, CONV_K - 1, CONV_WIDTH), F32)
        cstate_spec = pl.BlockSpec((1, CONV_K - 1, CONV_WIDTH), lambda t: (t // tps, 0, 0))
    out_shape = (kv_shape, kv_shape,
                 sds((n, SB_WIDTH), BF16), kbf_shape, sds((n, SB_WIDTH), BF16),
                 sds((n, D_MODEL), BF16), sds((n, D_MODEL), BF16), cstate_shape)
    out_specs = (kv_spec, kv_spec, row(SB_WIDTH), kbf_spec, row(SB_WIDTH),
                 row(D_MODEL), row(D_MODEL), cstate_spec)
    return pl.pallas_call(
        functools.partial(_proj_kernel, tiles_per_seq=tiles_per_seq, seq_in_tile=seq_in_tile),
        grid=grid, in_specs=in_specs, out_specs=out_specs, out_shape=out_shape,
        scratch_shapes=[pltpu.VMEM((tm + CONV_PAD, CONV_WIDTH), F32)],
        compiler_params=pltpu.CompilerParams(
            dimension_semantics=("arbitrary",), vmem_limit_bytes=VMEM_LIMIT),
        name="proj_sample" if sample else "proj_prompt",
    )(*args)


def _sb_block(q2, kw, vw, tri, carry, bias, k_is_t, v_is_t):
    z = _dot(q2, kw) if k_is_t else _dot_nt(q2, kw)
    if bias is not None:
        z = z + bias
    sp = jnp.maximum(z, 0.0) + jnp.log(1.0 + jnp.exp2(-jnp.abs(z))) * LOG2E
    tail = _dot(sp.astype(BF16), tri)
    a = jnp.exp2(z - tail + carry).astype(BF16)
    pv = _dot_nt(a, vw) if v_is_t else _dot(a, vw)
    return pv, carry - tail[:, 0:1]


def _stack_heads(qp):
    lane = lax.broadcasted_iota(jnp.int32, (1, LANES), 1)
    zero = jnp.zeros((), BF16)
    return jnp.concatenate([jnp.where(lane < HEAD_DIM, qp, zero),
                            jnp.where(lane >= HEAD_DIM, qp, zero)], axis=0)


def _unstack_heads(acc):
    lane = lax.broadcasted_iota(jnp.int32, (1, LANES), 1)
    half = acc.shape[0] // 2
    return jnp.where(lane < HEAD_DIM, acc[:half], acc[half:])


def _pair(p):
    return slice(p * LANES, (p + 1) * LANES)


def _sweep(q2, blocks, acc_ref, carry_ref, first):
    rows = q2.shape[0]
    ch = min(ATT_CHUNK, rows)
    mxv = None
    for r0 in range(0, rows, ch):
        rs = slice(r0, r0 + ch)
        carry = jnp.zeros((ch, 1), F32) if first else carry_ref[rs]
        pv = None
        for kw, vw, tri, bias_ref, k_is_t, v_is_t in blocks:
            bias = None if bias_ref is None else bias_ref[rs]
            pv_b, carry = _sb_block(q2[rs], kw(), vw(), tri(), carry, bias, k_is_t, v_is_t)
            pv = pv_b if pv is None else pv + pv_b
        acc_ref[rs] = pv if first else acc_ref[rs] + pv
        carry_ref[rs] = carry
        mxv = carry if mxv is None else jnp.maximum(mxv, carry)
    return jnp.max(mxv)


def _attn_prompt_kernel(q_ref, kt_ref, v_ref, tri_ref, bias_ref, o_ref, carry_ref, acc_ref):
    i = pl.program_id(1)
    tq = q_ref.shape[0]

    def block(p, j, bias):
        k0 = pl.multiple_of(j * tq, tq)
        return (lambda: kt_ref[0, j, _pair(p), :], lambda: v_ref[pl.ds(k0, tq), _pair(p)],
                lambda: tri_ref[...], bias, True, False)

    @pl.when(i == 0)
    def _():
        for p in range(N_PAIRS):
            q2 = _stack_heads(q_ref[:, _pair(p)])
            _sweep(q2, [block(p, i, bias_ref)], acc_ref.at[p], carry_ref.at[p], True)

    @pl.when(i > 0)
    def _():
        mx = []
        for p in range(N_PAIRS):
            q2 = _stack_heads(q_ref[:, _pair(p)])
            mx.append(_sweep(q2, [block(p, i, bias_ref), block(p, i - 1, None)],
                             acc_ref.at[p], carry_ref.at[p], True))
        for p in range(N_PAIRS):
            def body(state, p=p):
                j, _ = state
                q2 = _stack_heads(q_ref[:, _pair(p)])
                return j - 1, _sweep(q2, [block(p, j, None)], acc_ref.at[p], carry_ref.at[p], False)

            lax.while_loop(lambda s: (s[0] >= 0) & (s[1] > LOG2_ZERO), body, (i - 2, mx[p]))

    for p in range(N_PAIRS):
        o_ref[:, _pair(p)] = _unstack_heads(acc_ref[p]).astype(BF16)


def _attn_prompt_call(q_bf, kt_bf, v_bf, tri, bias, batch, seq):
    tq = ATT_BLOCK
    nq = seq // tq
    return pl.pallas_call(
        _attn_prompt_kernel,
        grid=(batch, nq),
        in_specs=[pl.BlockSpec((tq, SB_WIDTH), lambda b, i: (b * nq + i, 0)),
                  pl.BlockSpec((1, nq, SB_WIDTH, tq), lambda b, i: (b, 0, 0, 0),
                               pipeline_mode=pl.Buffered(1)),
                  pl.BlockSpec((seq, SB_WIDTH), lambda b, i: (b, 0), pipeline_mode=pl.Buffered(1)),
                  _resident(tri.shape), _resident(bias.shape)],
        out_specs=pl.BlockSpec((tq, SB_WIDTH), lambda b, i: (b * nq + i, 0)),
        out_shape=jax.ShapeDtypeStruct((batch * seq, SB_WIDTH), BF16),
        scratch_shapes=[pltpu.VMEM((N_PAIRS, 2 * tq, 1), F32),
                        pltpu.VMEM((N_PAIRS, 2 * tq, LANES), F32)],
        compiler_params=pltpu.CompilerParams(
            dimension_semantics=("arbitrary", "arbitrary"), vmem_limit_bytes=VMEM_LIMIT),
        name="attn_prompt",
    )(q_bf, kt_bf, v_bf, tri, bias)


def _attn_sample_kernel(q_ref, kn_ref, vn_ref, ck0_ref, cv0_ref, ck_any, cv_any, tri_ref, bias_ref,
                        o_ref, carry_ref, acc_ref, kpad_ref, vpad_ref, kbuf_ref, vbuf_ref, sem):
    b = pl.program_id(0)
    tq = q_ref.shape[0]
    tk = tri_ref.shape[0]
    n_win = ck_any.shape[2] // tk
    kpad_ref[...] = jnp.zeros_like(kpad_ref)
    vpad_ref[...] = jnp.zeros_like(vpad_ref)
    kpad_ref[0:tq, :] = kn_ref[...]
    vpad_ref[0:tq, :] = vn_ref[...]
    mx = []
    for p in range(N_PAIRS):
        q2 = _stack_heads(q_ref[:, _pair(p)])
        new = (lambda p=p: kpad_ref[:, _pair(p)], lambda p=p: vpad_ref[:, _pair(p)],
               lambda: tri_ref[0:LANES, 0:LANES], bias_ref, False, False)
        last = (lambda p=p: ck0_ref[0, _pair(p), :].astype(BF16),
                lambda p=p: cv0_ref[0, _pair(p), :].astype(BF16),
                lambda: tri_ref[...], None, True, True)
        mx.append(_sweep(q2, [new, last], acc_ref.at[p], carry_ref.at[p], True))
    for p in range(N_PAIRS):
        def window(src, dst, j, slot, p=p):
            k0 = pl.multiple_of(j * tk, tk)
            return pltpu.make_async_copy(src.at[b, _pair(p), pl.ds(k0, tk)], dst, sem.at[slot])

        def body(state, p=p, window=window):
            j, _ = state
            window(ck_any, kbuf_ref, j, 0).start()
            window(cv_any, vbuf_ref, j, 1).start()
            window(ck_any, kbuf_ref, j, 0).wait()
            window(cv_any, vbuf_ref, j, 1).wait()
            q2 = _stack_heads(q_ref[:, _pair(p)])
            win = (lambda: kbuf_ref[...].astype(BF16), lambda: vbuf_ref[...].astype(BF16),
                   lambda: tri_ref[...], None, True, True)
            return j - 1, _sweep(q2, [win], acc_ref.at[p], carry_ref.at[p], False)

        lax.while_loop(lambda s: (s[0] >= 0) & (s[1] > LOG2_ZERO), body, (n_win - 2, mx[p]))
    for p in range(N_PAIRS):
        o_ref[:, _pair(p)] = _unstack_heads(acc_ref[p]).astype(BF16)


def _attn_sample_call(q_bf, k_bf, v_bf, ckt, cvt, tri, bias, batch, seq):
    past = ckt.shape[2]
    tk = tri.shape[0]
    n_win = past // tk
    row = pl.BlockSpec((seq, SB_WIDTH), lambda b: (b, 0))
    last = pl.BlockSpec((1, SB_WIDTH, tk), lambda b: (b, 0, n_win - 1))
    hbm = pl.BlockSpec(memory_space=pl.ANY)
    return pl.pallas_call(
        _attn_sample_kernel,
        grid=(batch,),
        in_specs=[row, row, row, last, last, hbm, hbm, _resident(tri.shape), _resident(bias.shape)],
        out_specs=row,
        out_shape=jax.ShapeDtypeStruct((batch * seq, SB_WIDTH), BF16),
        scratch_shapes=[pltpu.VMEM((N_PAIRS, 2 * seq, 1), F32),
                        pltpu.VMEM((N_PAIRS, 2 * seq, LANES), F32),
                        pltpu.VMEM((LANES, SB_WIDTH), BF16), pltpu.VMEM((LANES, SB_WIDTH), BF16),
                        pltpu.VMEM((LANES, tk), F32), pltpu.VMEM((LANES, tk), F32),
                        pltpu.SemaphoreType.DMA((2,))],
        compiler_params=pltpu.CompilerParams(
            dimension_semantics=("arbitrary",), vmem_limit_bytes=VMEM_LIMIT),
        name="attn_sample",
    )(q_bf, k_bf, v_bf, ckt, cvt, ckt, cvt, tri, bias)


def _post_kernel(x_ref, att_ref, sga_ref, gyc_ref, p_ref,
                 wa_ref, wo_ref, gf_ref, wu_ref, wd_ref, gp_ref, wg_ref, wp_ref, gl_ref,
                 y_ref):
    tm = x_ref.shape[0]
    groups = [slice(r0, r0 + tm // POST_GROUPS) for r0 in range(0, tm, tm // POST_GROUPS)]
    each = lambda fn, *cols: [fn(*vals) for vals in zip(*cols)]
    ya = each(lambda rs: _dot(att_ref[rs], wa_ref[...]), groups)
    merged = each(lambda rs, ya: sga_ref[rs].astype(F32) * ya + gyc_ref[rs].astype(F32), groups, ya)
    x1 = each(lambda rs, m: x_ref[rs] + _dot(m.astype(BF16), wo_ref[...]), groups, merged)
    h2 = each(lambda x: _rms(x, gf_ref[...]).astype(BF16), x1)
    f = each(jnp.zeros_like, x1)
    for c in range(0, D_FF, FF_CHUNK):
        up = each(lambda h: jnp.maximum(_dot(h, wu_ref[:, c:c + FF_CHUNK]), 0.0), h2)
        f = each(lambda f, u: f + _dot((u * u).astype(BF16), wd_ref[c:c + FF_CHUNK, :]), f, up)
    x2 = each(lambda a, b: a + b, x1, f)
    h3 = each(lambda x: _rms(x, gp_ref[...]).astype(BF16), x2)
    gate = each(lambda h: jax.nn.sigmoid(_dot(h, wg_ref[...])), h3)
    x3 = each(lambda rs, x, g: x + g * _dot(p_ref[rs].astype(BF16), wp_ref[...]), groups, x2, gate)
    for rs, x in zip(groups, x3):
        y_ref[rs] = _rms(x, gl_ref[...])


def _post_call(x, att, sga, gyc, p, wa, wo, gf, wu, wd, gp, wg, wp, gl, name):
    n = x.shape[0]
    tm = min(ROW_TILE, n)
    row = lambda w: pl.BlockSpec((tm, w), lambda t: (t, 0))
    vec = lambda g: g.reshape(1, D_MODEL)
    return pl.pallas_call(
        _post_kernel,
        grid=(n // tm,),
        in_specs=[row(D_MODEL), row(SB_WIDTH), row(D_MODEL), row(D_MODEL), row(PLE_DIM),
                  _resident(wa.shape), _resident(wo.shape),
                  _resident((1, D_MODEL)), _resident(wu.shape), _resident(wd.shape),
                  _resident((1, D_MODEL)), _resident(wg.shape), _resident(wp.shape),
                  _resident((1, D_MODEL))],
        out_specs=row(D_MODEL),
        out_shape=jax.ShapeDtypeStruct((n, D_MODEL), F32),
        compiler_params=pltpu.CompilerParams(
            dimension_semantics=("arbitrary",), vmem_limit_bytes=VMEM_LIMIT),
        name=name,
    )(x, att, sga, gyc, p, wa, wo, vec(gf), wu, wd, vec(gp), wg, wp, vec(gl))


def _causal_bias(rows, cols, period):
    r = lax.broadcasted_iota(jnp.int32, (rows, cols), 0) % period
    c = lax.broadcasted_iota(jnp.int32, (rows, cols), 1)
    return jnp.where(c < r, 0.0, -jnp.inf).astype(F32)


def kernel(x_prompt, x_sample, p_prompt, p_sample, cache_k, cache_v, cache_conv,
           g_mix, w_in, conv_w, w_attn_out, w_conv_out, w_o,
           g_ffn, w_up, w_down, g_ple, w_ple_gate, w_ple, g_final):
    depth = w_in.shape[0]
    assert depth == 1
    b, t, _ = x_prompt.shape
    db, dt, _ = x_sample.shape
    past = cache_k.shape[2]
    bf = lambda w: w[0].astype(BF16)
    w_in_b, wa, wc, wo = bf(w_in), bf(w_attn_out), bf(w_conv_out), bf(w_o)
    wu, wd, wg, wp = bf(w_up), bf(w_down), bf(w_ple_gate), bf(w_ple)
    r = lax.broadcasted_iota(jnp.int32, (ATT_BLOCK, ATT_BLOCK), 0)
    c = lax.broadcasted_iota(jnp.int32, (ATT_BLOCK, ATT_BLOCK), 1)
    tri = (r >= c).astype(BF16)
    hd = (N_HEADS, HEAD_DIM)

    xp = x_prompt.reshape(b * t, D_MODEL)
    kpt, vpt, q_bf, kt_bf, v_bf, sga, gyc, conv_p = _proj_call(
        xp, g_mix[0], w_in_b, conv_w[0], wc, t)
    att = _attn_prompt_call(q_bf, kt_bf, v_bf, tri,
                            _causal_bias(2 * ATT_BLOCK, ATT_BLOCK, ATT_BLOCK), b, t)
    yp = _post_call(xp, att, sga, gyc, p_prompt[0].reshape(b * t, PLE_DIM),
                    wa, wo, g_ffn[0], wu, wd, g_ple[0], wg, wp, g_final, "post_prompt")
    kp = kpt.reshape(b, *hd, t).transpose(0, 3, 1, 2)
    vp = vpt.reshape(b, *hd, t).transpose(0, 3, 1, 2)

    xs = x_sample.reshape(db * dt, D_MODEL)
    buf = cache_conv[0]
    zeros = jnp.zeros((db, dt, CONV_WIDTH), F32)
    e1 = zeros.at[:, 0].set(buf[:, 1]).reshape(db * dt, CONV_WIDTH)
    e2 = zeros.at[:, 0].set(buf[:, 0]).at[:, 1].set(buf[:, 1]).reshape(db * dt, CONV_WIDTH)
    ks, vs, q_bf, k_bf, v_bf, sga, gyc, u_s = _proj_call(
        xs, g_mix[0], w_in_b, conv_w[0], wc, dt, hist=(e1, e2))
    ckt = cache_k[0].transpose(0, 2, 3, 1).reshape(db, SB_WIDTH, past)
    cvt = cache_v[0].transpose(0, 2, 3, 1).reshape(db, SB_WIDTH, past)
    att = _attn_sample_call(q_bf, k_bf, v_bf, ckt, cvt, tri, _causal_bias(2 * dt, LANES, dt), db, dt)
    ys = _post_call(xs, att, sga, gyc, p_sample[0].reshape(db * dt, PLE_DIM),
                    wa, wo, g_ffn[0], wu, wd, g_ple[0], wg, wp, g_final, "post_sample")
    conv_s = u_s.reshape(db, dt, CONV_WIDTH)[:, dt - (CONV_K - 1):]

    return (yp.reshape(b, t, D_MODEL), ys.reshape(db, dt, D_MODEL),
            kp[None], vp[None], conv_p[None],
            ks.reshape(1, db, dt, *hd), vs.reshape(1, db, dt, *hd), conv_s[None])
```

```python
import functools
import math

import jax
import jax.numpy as jnp
from jax import lax
from jax.experimental import pallas as pl
from jax.experimental.pallas import tpu as pltpu

F32 = jnp.float32
BF16 = jnp.bfloat16

D_MODEL = 1024
N_HEADS = 8
HEAD_DIM = 64
SB_WIDTH = N_HEADS * HEAD_DIM
CONV_WIDTH = 512
CONV_K = 3
PLE_DIM = 256
D_FF = 4 * D_MODEL
EPS = 1e-6
N_PROJ = 3 * SB_WIDTH + 3 * CONV_WIDTH + 2 * D_MODEL

LANES = 128
HEADS_PER_TILE = LANES // HEAD_DIM
N_PAIRS = N_HEADS // HEADS_PER_TILE
ROW_TILE = 512
ATT_BLOCK = 256
ATT_CHUNK = 512
FF_CHUNK = 1024
POST_GROUPS = 2
CONV_PAD = 8
LOG2E = 1.4426950408889634
LOG2_ZERO = -126.0
KEY_TILE = LANES
VMEM_LIMIT = 52 * 1024 * 1024


def _resident(shape):
    nd = len(shape)
    return pl.BlockSpec(shape, lambda *_: (0,) * nd, pipeline_mode=pl.Buffered(1))


def _rms(x, g):
    return (x * lax.rsqrt(jnp.mean(x * x, axis=-1, keepdims=True) + EPS)) * g


def _dot(a, b):
    return jnp.dot(a, b, preferred_element_type=F32)


def _dot_nt(a, b):
    return lax.dot_general(a, b, (((1,), (1,)), ((), ())), preferred_element_type=F32)


def _proj_kernel(*refs, tiles_per_seq, seq_in_tile):
    prompt = seq_in_tile is None
    if prompt:
        (x_ref, g_ref, w_ref, cw_ref, wc_ref,
         k_out, v_out, q_bf, k_bf, v_bf, sga, gyc, cstate, s_ref) = refs
    else:
        (x_ref, g_ref, w_ref, cw_ref, wc_ref, e1_ref, e2_ref,
         k_out, v_out, q_bf, k_bf, v_bf, sga, gyc, cstate, s_ref) = refs
    tm = x_ref.shape[0]
    hb = _rms(x_ref[...], g_ref[...]).astype(BF16)
    o_q, o_k, o_v = 0, SB_WIDTH, 2 * SB_WIDTH
    o_cb = 3 * SB_WIDTH
    o_cc, o_cx = o_cb + CONV_WIDTH, o_cb + 2 * CONV_WIDTH
    o_ga = o_cb + 3 * CONV_WIDTH
    o_gc = o_ga + D_MODEL

    def proj(lo, width):
        return _dot(hb, w_ref[:, lo:lo + width])

    cb = proj(o_cb, CONV_WIDTH)
    u = proj(o_cc, CONV_WIDTH) * proj(o_cx, CONV_WIDTH)
    if prompt:
        @pl.when(pl.program_id(0) % tiles_per_seq == 0)
        def _():
            s_ref[0:CONV_PAD, :] = jnp.zeros((CONV_PAD, CONV_WIDTH), F32)
    else:
        s_ref[0:CONV_PAD, :] = jnp.zeros((CONV_PAD, CONV_WIDTH), F32)
    s_ref[CONV_PAD:CONV_PAD + tm, :] = u
    prev1 = s_ref[CONV_PAD - 1:CONV_PAD - 1 + tm, :]
    prev2 = s_ref[CONV_PAD - 2:CONV_PAD - 2 + tm, :]
    if prompt:
        s_ref[CONV_PAD - 2:CONV_PAD, :] = u[tm - 2:tm, :]
        cstate[0] = u[tm - 2:tm, :]
    else:
        r = lax.broadcasted_iota(jnp.int32, (tm, 1), 0) % seq_in_tile
        prev1 = jnp.where(r < 1, e1_ref[...], prev1)
        prev2 = jnp.where(r < 2, e2_ref[...], prev2)
        cstate[...] = u
    conv = cw_ref[0:1, :] * prev2 + cw_ref[1:2, :] * prev1 + cw_ref[2:3, :] * u
    cpre = (cb * conv).astype(BF16)

    sga[...] = jax.nn.sigmoid(proj(o_ga, D_MODEL)).astype(BF16)
    gyc[...] = (jax.nn.sigmoid(proj(o_gc, D_MODEL)) * _dot(cpre, wc_ref[...])).astype(BF16)

    k = proj(o_k, SB_WIDTH)
    v = proj(o_v, SB_WIDTH)
    if prompt:
        kt = k.T
        k_out[0] = kt
        for j in range(tm // KEY_TILE):
            k_bf[0, j] = kt[:, j * KEY_TILE:(j + 1) * KEY_TILE].astype(BF16)
        v_out[0] = v.T
    else:
        k_out[...] = k
        k_bf[...] = k.astype(BF16)
        v_out[...] = v
    v_bf[...] = v.astype(BF16)
    q_bf[...] = (proj(o_q, SB_WIDTH) * (LOG2E / math.sqrt(HEAD_DIM))).astype(BF16)


def _proj_call(x, g, w_in, conv_w, w_conv_out, seq_len, hist=None):
    n = x.shape[0]
    tm = min(ROW_TILE, n)
    grid = (n // tm,)
    sample = hist is not None
    sds = jax.ShapeDtypeStruct
    row = lambda w: pl.BlockSpec((tm, w), lambda t: (t, 0))
    in_specs = [row(D_MODEL), _resident((1, D_MODEL)), _resident((D_MODEL, N_PROJ)),
                _resident((CONV_K, CONV_WIDTH)), _resident((CONV_WIDTH, D_MODEL))]
    args = [x, g.reshape(1, D_MODEL), w_in, conv_w, w_conv_out]
    if sample:
        in_specs += [row(CONV_WIDTH), row(CONV_WIDTH)]
        args += list(hist)
        tiles_per_seq, seq_in_tile = None, seq_len
        kv_shape = sds((n, SB_WIDTH), F32)
        kv_spec = row(SB_WIDTH)
        kbf_shape = sds((n, SB_WIDTH), BF16)
        kbf_spec = row(SB_WIDTH)
        cstate_shape = sds((n, CONV_WIDTH), F32)
        cstate_spec = row(CONV_WIDTH)
    else:
        tps = seq_len // tm
        nb = tm // KEY_TILE
        tiles_per_seq, seq_in_tile = tps, None
        batch = n // seq_len
        kv_shape = sds((batch, SB_WIDTH, seq_len), F32)
        kv_spec = pl.BlockSpec((1, SB_WIDTH, tm), lambda t: (t // tps, 0, t % tps))
        kbf_shape = sds((batch, seq_len // KEY_TILE, SB_WIDTH, KEY_TILE), BF16)
        kbf_spec = pl.BlockSpec((1, nb, SB_WIDTH, KEY_TILE), lambda t: (t // tps, t % tps, 0, 0))
        cstate_shape = sds((batch, CONV_K - 1, CONV_WIDTH), F32)
        cstate_spec = pl.BlockSpec((1, CONV_K - 1, CONV_WIDTH), lambda t: (t // tps, 0, 0))
    out_shape = (kv_shape, kv_shape,
                 sds((n, SB_WIDTH), BF16), kbf_shape, sds((n, SB_WIDTH), BF16),
                 sds((n, D_MODEL), BF16), sds((n, D_MODEL), BF16), cstate_shape)
    out_specs = (kv_spec, kv_spec, row(SB_WIDTH), kbf_spec, row(SB_WIDTH),
                 row(D_MODEL), row(D_MODEL), cstate_spec)
    return pl.pallas_call(
        functools.partial(_proj_kernel, tiles_per_seq=tiles_per_seq, seq_in_tile=seq_in_tile),
        grid=grid, in_specs=in_specs, out_specs=out_specs, out_shape=out_shape,
        scratch_shapes=[pltpu.VMEM((tm + CONV_PAD, CONV_WIDTH), F32)],
        compiler_params=pltpu.CompilerParams(
            dimension_semantics=("arbitrary",), vmem_limit_bytes=VMEM_LIMIT),
        name="proj_sample" if sample else "proj_prompt",
    )(*args)


def _softplus2(z):
    return jnp.maximum(z, 0.0) + jnp.log(1.0 + jnp.exp2(-jnp.abs(z))) * LOG2E


def _sb_block(q2, kw, vw, tri, carry, bias, k_is_t, v_is_t):
    z = _dot(q2, kw) if k_is_t else _dot_nt(q2, kw)
    if bias is not None:
        z = z + bias
    return _finish_block(z, vw, tri, carry, v_is_t)


def _finish_block(z, vw, tri, carry, v_is_t=False):
    tail = _dot(_softplus2(z).astype(BF16), tri)
    a = jnp.exp2(z - tail + carry).astype(BF16)
    pv = _dot_nt(a, vw) if v_is_t else _dot(a, vw)
    return pv, carry - tail[:, 0:1]


def _stack_heads(qp):
    lane = lax.broadcasted_iota(jnp.int32, (1, LANES), 1)
    zero = jnp.zeros((), BF16)
    return jnp.concatenate([jnp.where(lane < HEAD_DIM, qp, zero),
                            jnp.where(lane >= HEAD_DIM, qp, zero)], axis=0)


def _unstack_heads(acc):
    lane = lax.broadcasted_iota(jnp.int32, (1, LANES), 1)
    half = acc.shape[0] // 2
    return jnp.where(lane < HEAD_DIM, acc[:half], acc[half:])


def _pair(p):
    return slice(p * LANES, (p + 1) * LANES)


def _sweep(q2, blocks, acc_ref, carry_ref, first):
    rows = q2.shape[0]
    ch = min(ATT_CHUNK, rows)
    mxv = None
    for r0 in range(0, rows, ch):
        rs = slice(r0, r0 + ch)
        carry = jnp.zeros((ch, 1), F32) if first else carry_ref[rs]
        pv = None
        for kw, vw, tri, bias_ref, k_is_t, v_is_t in blocks:
            bias = None if bias_ref is None else bias_ref[rs]
            pv_b, carry = _sb_block(q2[rs], kw(), vw(), tri(), carry, bias, k_is_t, v_is_t)
            pv = pv_b if pv is None else pv + pv_b
        acc_ref[rs] = pv if first else acc_ref[rs] + pv
        carry_ref[rs] = carry
        mxv = carry if mxv is None else jnp.maximum(mxv, carry)
    return jnp.max(mxv)


def _attn_prompt_kernel(q_ref, kt_ref, v_ref, tri_ref, bias_ref, o_ref, carry_ref, acc_ref):
    i = pl.program_id(1)
    n_sub = q_ref.shape[0] // KEY_TILE
    rows = HEADS_PER_TILE * KEY_TILE
    causal, left_off = bias_ref[0], bias_ref[1]
    inst = [(p, s) for p in range(N_PAIRS) for s in range(n_sub)]

    def q2_of(p, s):
        return _stack_heads(q_ref[s * KEY_TILE:(s + 1) * KEY_TILE, _pair(p)])

    def window(p, jr):
        jl = jnp.maximum(jr - 1, 0)
        kw = jnp.concatenate([kt_ref[0, jl, _pair(p), :], kt_ref[0, jr, _pair(p), :]], axis=1)
        rows_of = lambda j: pl.ds(pl.multiple_of(j * KEY_TILE, KEY_TILE), KEY_TILE)
        vw = jnp.concatenate([v_ref[rows_of(jl), _pair(p)], v_ref[rows_of(jr), _pair(p)]], axis=0)
        return kw, vw

    first_bias = jnp.where(i == 0, causal + left_off, causal)
    zs, vws = [], []
    for p, s in inst:
        kw, vw = window(p, n_sub * i + s)
        zs.append(_dot(q2_of(p, s), kw) + (first_bias if s == 0 else causal))
        vws.append(vw)
    sps = jnp.concatenate([_softplus2(z).astype(BF16) for z in zs], axis=0)
    tails = _dot(sps, tri_ref[...])
    mx = []
    for n, (p, s) in enumerate(inst):
        tail = tails[n * rows:(n + 1) * rows]
        a = jnp.exp2(zs[n] - tail).astype(BF16)
        acc_ref[p, s] = _dot(a, vws[n])
        carry = -tail[:, 0:1]
        carry_ref[p, s] = carry
        mx.append(jnp.max(carry))

    for n, (p, s) in enumerate(inst):
        def body(state, p=p, s=s):
            jr, _ = state
            kw, vw = window(p, jr)
            z = _dot(q2_of(p, s), kw) + jnp.where(jr >= 1, jnp.zeros_like(left_off), left_off)
            pv, carry = _finish_block(z, vw, tri_ref[...], carry_ref[p, s])
            acc_ref[p, s] += pv
            carry_ref[p, s] = carry
            return jr - 2, jnp.max(carry)

        lax.while_loop(lambda st: (st[0] >= 0) & (st[1] > LOG2_ZERO), body,
                       (n_sub * i + s - 2, mx[n]))

    for p, s in inst:
        o_ref[s * KEY_TILE:(s + 1) * KEY_TILE, _pair(p)] = _unstack_heads(acc_ref[p, s]).astype(BF16)


def _attn_prompt_call(q_bf, kt_bf, v_bf, tri, bias, batch, seq):
    tq = ATT_BLOCK
    nq = seq // tq
    n_sub = tq // KEY_TILE
    rows = HEADS_PER_TILE * KEY_TILE
    return pl.pallas_call(
        _attn_prompt_kernel,
        grid=(batch, nq),
        in_specs=[pl.BlockSpec((tq, SB_WIDTH), lambda b, i: (b * nq + i, 0)),
                  pl.BlockSpec((1, seq // KEY_TILE, SB_WIDTH, KEY_TILE), lambda b, i: (b, 0, 0, 0)),
                  pl.BlockSpec((seq, SB_WIDTH), lambda b, i: (b, 0)),
                  _resident(tri.shape), _resident(bias.shape)],
        out_specs=pl.BlockSpec((tq, SB_WIDTH), lambda b, i: (b * nq + i, 0)),
        out_shape=jax.ShapeDtypeStruct((batch * seq, SB_WIDTH), BF16),
        scratch_shapes=[pltpu.VMEM((N_PAIRS, n_sub, rows, 1), F32),
                        pltpu.VMEM((N_PAIRS, n_sub, rows, LANES), F32)],
        compiler_params=pltpu.CompilerParams(
            dimension_semantics=("arbitrary", "arbitrary"), vmem_limit_bytes=VMEM_LIMIT),
        name="attn_prompt",
    )(q_bf, kt_bf, v_bf, tri, bias)


def _attn_sample_kernel(q_ref, kn_ref, vn_ref, ck0_ref, cv0_ref, ck_any, cv_any, tri_ref, bias_ref,
                        o_ref, carry_ref, acc_ref, kpad_ref, vpad_ref, kbuf_ref, vbuf_ref, sem):
    b = pl.program_id(0)
    tq = q_ref.shape[0]
    tk = tri_ref.shape[0]
    n_win = ck_any.shape[2] // tk
    kpad_ref[...] = jnp.zeros_like(kpad_ref)
    vpad_ref[...] = jnp.zeros_like(vpad_ref)
    kpad_ref[0:tq, :] = kn_ref[...]
    vpad_ref[0:tq, :] = vn_ref[...]
    mx = []
    for p in range(N_PAIRS):
        q2 = _stack_heads(q_ref[:, _pair(p)])
        new = (lambda p=p: kpad_ref[:, _pair(p)], lambda p=p: vpad_ref[:, _pair(p)],
               lambda: tri_ref[0:LANES, 0:LANES], bias_ref, False, False)
        last = (lambda p=p: ck0_ref[0, _pair(p), :].astype(BF16),
                lambda p=p: cv0_ref[0, _pair(p), :].astype(BF16),
                lambda: tri_ref[...], None, True, True)
        mx.append(_sweep(q2, [new, last], acc_ref.at[p], carry_ref.at[p], True))
    for p in range(N_PAIRS):
        def window(src, dst, j, slot, p=p):
            k0 = pl.multiple_of(j * tk, tk)
            return pltpu.make_async_copy(src.at[b, _pair(p), pl.ds(k0, tk)], dst, sem.at[slot])

        def body(state, p=p, window=window):
            j, _ = state
            window(ck_any, kbuf_ref, j, 0).start()
            window(cv_any, vbuf_ref, j, 1).start()
            window(ck_any, kbuf_ref, j, 0).wait()
            window(cv_any, vbuf_ref, j, 1).wait()
            q2 = _stack_heads(q_ref[:, _pair(p)])
            win = (lambda: kbuf_ref[...].astype(BF16), lambda: vbuf_ref[...].astype(BF16),
                   lambda: tri_ref[...], None, True, True)
            return j - 1, _sweep(q2, [win], acc_ref.at[p], carry_ref.at[p], False)

        lax.while_loop(lambda s: (s[0] >= 0) & (s[1] > LOG2_ZERO), body, (n_win - 2, mx[p]))
    for p in range(N_PAIRS):
        o_ref[:, _pair(p)] = _unstack_heads(acc_ref[p]).astype(BF16)


def _attn_sample_call(q_bf, k_bf, v_bf, ckt, cvt, tri, bias, batch, seq):
    past = ckt.shape[2]
    tk = tri.shape[0]
    n_win = past // tk
    row = pl.BlockSpec((seq, SB_WIDTH), lambda b: (b, 0))
    last = pl.BlockSpec((1, SB_WIDTH, tk), lambda b: (b, 0, n_win - 1))
    hbm = pl.BlockSpec(memory_space=pl.ANY)
    return pl.pallas_call(
        _attn_sample_kernel,
        grid=(batch,),
        in_specs=[row, row, row, last, last, hbm, hbm, _resident(tri.shape), _resident(bias.shape)],
        out_specs=row,
        out_shape=jax.ShapeDtypeStruct((batch * seq, SB_WIDTH), BF16),
        scratch_shapes=[pltpu.VMEM((N_PAIRS, 2 * seq, 1), F32),
                        pltpu.VMEM((N_PAIRS, 2 * seq, LANES), F32),
                        pltpu.VMEM((LANES, SB_WIDTH), BF16), pltpu.VMEM((LANES, SB_WIDTH), BF16),
                        pltpu.VMEM((LANES, tk), F32), pltpu.VMEM((LANES, tk), F32),
                        pltpu.SemaphoreType.DMA((2,))],
        compiler_params=pltpu.CompilerParams(
            dimension_semantics=("arbitrary",), vmem_limit_bytes=VMEM_LIMIT),
        name="attn_sample",
    )(q_bf, k_bf, v_bf, ckt, cvt, ckt, cvt, tri, bias)


def _post_kernel(x_ref, att_ref, sga_ref, gyc_ref, p_ref,
                 wa_ref, wo_ref, gf_ref, wu_ref, wd_ref, gp_ref, wg_ref, wp_ref, gl_ref,
                 y_ref):
    tm = x_ref.shape[0]
    groups = [slice(r0, r0 + tm // POST_GROUPS) for r0 in range(0, tm, tm // POST_GROUPS)]
    each = lambda fn, *cols: [fn(*vals) for vals in zip(*cols)]
    ya = each(lambda rs: _dot(att_ref[rs], wa_ref[...]), groups)
    merged = each(lambda rs, ya: sga_ref[rs].astype(F32) * ya + gyc_ref[rs].astype(F32), groups, ya)
    x1 = each(lambda rs, m: x_ref[rs] + _dot(m.astype(BF16), wo_ref[...]), groups, merged)
    h2 = each(lambda x: _rms(x, gf_ref[...]).astype(BF16), x1)
    f = each(jnp.zeros_like, x1)
    for c in range(0, D_FF, FF_CHUNK):
        up = each(lambda h: jnp.maximum(_dot(h, wu_ref[:, c:c + FF_CHUNK]), 0.0), h2)
        f = each(lambda f, u: f + _dot((u * u).astype(BF16), wd_ref[c:c + FF_CHUNK, :]), f, up)
    x2 = each(lambda a, b: a + b, x1, f)
    h3 = each(lambda x: _rms(x, gp_ref[...]).astype(BF16), x2)
    gate = each(lambda h: jax.nn.sigmoid(_dot(h, wg_ref[...])), h3)
    x3 = each(lambda rs, x, g: x + g * _dot(p_ref[rs].astype(BF16), wp_ref[...]), groups, x2, gate)
    for rs, x in zip(groups, x3):
        y_ref[rs] = _rms(x, gl_ref[...])


def _post_call(x, att, sga, gyc, p, wa, wo, gf, wu, wd, gp, wg, wp, gl, name):
    n = x.shape[0]
    tm = min(ROW_TILE, n)
    row = lambda w: pl.BlockSpec((tm, w), lambda t: (t, 0))
    vec = lambda g: g.reshape(1, D_MODEL)
    return pl.pallas_call(
        _post_kernel,
        grid=(n // tm,),
        in_specs=[row(D_MODEL), row(SB_WIDTH), row(D_MODEL), row(D_MODEL), row(PLE_DIM),
                  _resident(wa.shape), _resident(wo.shape),
                  _resident((1, D_MODEL)), _resident(wu.shape), _resident(wd.shape),
                  _resident((1, D_MODEL)), _resident(wg.shape), _resident(wp.shape),
                  _resident((1, D_MODEL))],
        out_specs=row(D_MODEL),
        out_shape=jax.ShapeDtypeStruct((n, D_MODEL), F32),
        compiler_params=pltpu.CompilerParams(
            dimension_semantics=("arbitrary",), vmem_limit_bytes=VMEM_LIMIT),
        name=name,
    )(x, att, sga, gyc, p, wa, wo, vec(gf), wu, wd, vec(gp), wg, wp, vec(gl))


def _causal_bias(rows, cols, period, first_query=0):
    r = lax.broadcasted_iota(jnp.int32, (rows, cols), 0) % period + first_query
    c = lax.broadcasted_iota(jnp.int32, (rows, cols), 1)
    return jnp.where(c < r, 0.0, -jnp.inf).astype(F32)


def _prompt_bias():
    rows, cols = HEADS_PER_TILE * KEY_TILE, 2 * KEY_TILE
    c = lax.broadcasted_iota(jnp.int32, (rows, cols), 1)
    left_off = jnp.where(c < KEY_TILE, -jnp.inf, 0.0).astype(F32)
    return jnp.stack([_causal_bias(rows, cols, KEY_TILE, KEY_TILE), left_off])


def kernel(x_prompt, x_sample, p_prompt, p_sample, cache_k, cache_v, cache_conv,
           g_mix, w_in, conv_w, w_attn_out, w_conv_out, w_o,
           g_ffn, w_up, w_down, g_ple, w_ple_gate, w_ple, g_final):
    depth = w_in.shape[0]
    assert depth == 1
    b, t, _ = x_prompt.shape
    db, dt, _ = x_sample.shape
    past = cache_k.shape[2]
    bf = lambda w: w[0].astype(BF16)
    w_in_b, wa, wc, wo = bf(w_in), bf(w_attn_out), bf(w_conv_out), bf(w_o)
    wu, wd, wg, wp = bf(w_up), bf(w_down), bf(w_ple_gate), bf(w_ple)
    r = lax.broadcasted_iota(jnp.int32, (ATT_BLOCK, ATT_BLOCK), 0)
    c = lax.broadcasted_iota(jnp.int32, (ATT_BLOCK, ATT_BLOCK), 1)
    tri = (r >= c).astype(BF16)
    hd = (N_HEADS, HEAD_DIM)

    xp = x_prompt.reshape(b * t, D_MODEL)
    kpt, vpt, q_bf, kt_bf, v_bf, sga, gyc, conv_p = _proj_call(
        xp, g_mix[0], w_in_b, conv_w[0], wc, t)
    att = _attn_prompt_call(q_bf, kt_bf, v_bf, tri, _prompt_bias(), b, t)
    yp = _post_call(xp, att, sga, gyc, p_prompt[0].reshape(b * t, PLE_DIM),
                    wa, wo, g_ffn[0], wu, wd, g_ple[0], wg, wp, g_final, "post_prompt")
    kp = kpt.reshape(b, *hd, t).transpose(0, 3, 1, 2)
    vp = vpt.reshape(b, *hd, t).transpose(0, 3, 1, 2)

    xs = x_sample.reshape(db * dt, D_MODEL)
    buf = cache_conv[0]
    zeros = jnp.zeros((db, dt, CONV_WIDTH), F32)
    e1 = zeros.at[:, 0].set(buf[:, 1]).reshape(db * dt, CONV_WIDTH)
    e2 = zeros.at[:, 0].set(buf[:, 0]).at[:, 1].set(buf[:, 1]).reshape(db * dt, CONV_WIDTH)
    ks, vs, q_bf, k_bf, v_bf, sga, gyc, u_s = _proj_call(
        xs, g_mix[0], w_in_b, conv_w[0], wc, dt, hist=(e1, e2))
    ckt = cache_k[0].transpose(0, 2, 3, 1).reshape(db, SB_WIDTH, past)
    cvt = cache_v[0].transpose(0, 2, 3, 1).reshape(db, SB_WIDTH, past)
    att = _attn_sample_call(q_bf, k_bf, v_bf, ckt, cvt, tri, _causal_bias(2 * dt, LANES, dt), db, dt)
    ys = _post_call(xs, att, sga, gyc, p_sample[0].reshape(db * dt, PLE_DIM),
                    wa, wo, g_ffn[0], wu, wd, g_ple[0], wg, wp, g_final, "post_sample")
    conv_s = u_s.reshape(db, dt, CONV_WIDTH)[:, dt - (CONV_K - 1):]

    return (yp.reshape(b, t, D_MODEL), ys.reshape(db, dt, D_MODEL),
            kp[None], vp[None], conv_p[None],
            ks.reshape(1, db, dt, *hd), vs.reshape(1, db, dt, *hd), conv_s[None])
```

```python
import functools
import math

import jax
import jax.numpy as jnp
from jax import lax
from jax.experimental import pallas as pl
from jax.experimental.pallas import tpu as pltpu

F32 = jnp.float32
BF16 = jnp.bfloat16

D_MODEL = 1024
N_HEADS = 8
HEAD_DIM = 64
SB_WIDTH = N_HEADS * HEAD_DIM
CONV_WIDTH = 512
CONV_K = 3
PLE_DIM = 256
D_FF = 4 * D_MODEL
EPS = 1e-6
N_PROJ = 3 * SB_WIDTH + 3 * CONV_WIDTH + 2 * D_MODEL

LANES = 128
HEADS_PER_TILE = LANES // HEAD_DIM
N_PAIRS = N_HEADS // HEADS_PER_TILE
ROW_TILE = 512
ATT_BLOCK = 256
ATT_ROWS = 512
FF_CHUNK = 1024
POST_GROUPS = 2
CONV_PAD = 8
LOG2E = 1.4426950408889634
LOG2_ZERO = -126.0
KEY_TILE = LANES
VMEM_LIMIT = 52 * 1024 * 1024


def _resident(shape):
    nd = len(shape)
    return pl.BlockSpec(shape, lambda *_: (0,) * nd, pipeline_mode=pl.Buffered(1))


def _rms(x, g):
    return (x * lax.rsqrt(jnp.mean(x * x, axis=-1, keepdims=True) + EPS)) * g


def _dot(a, b):
    return jnp.dot(a, b, preferred_element_type=F32)


def _dot_nt(a, b):
    return lax.dot_general(a, b, (((1,), (1,)), ((), ())), preferred_element_type=F32)


def _proj_kernel(*refs, tiles_per_seq, seq_in_tile):
    prompt = seq_in_tile is None
    if prompt:
        (x_ref, g_ref, w_ref, cw_ref, wc_ref,
         k_out, v_out, q_bf, k_bf, v_bf, sga, gyc, cstate, s_ref) = refs
    else:
        (x_ref, g_ref, w_ref, cw_ref, wc_ref, e1_ref, e2_ref,
         k_out, v_out, q_bf, k_bf, v_bf, sga, gyc, cstate, s_ref) = refs
    tm = x_ref.shape[0]
    hb = _rms(x_ref[...], g_ref[...]).astype(BF16)
    o_q, o_k, o_v = 0, SB_WIDTH, 2 * SB_WIDTH
    o_cb = 3 * SB_WIDTH
    o_cc, o_cx = o_cb + CONV_WIDTH, o_cb + 2 * CONV_WIDTH
    o_ga = o_cb + 3 * CONV_WIDTH
    o_gc = o_ga + D_MODEL

    def proj(lo, width):
        return _dot(hb, w_ref[:, lo:lo + width])

    cb = proj(o_cb, CONV_WIDTH)
    u = proj(o_cc, CONV_WIDTH) * proj(o_cx, CONV_WIDTH)
    if prompt:
        @pl.when(pl.program_id(0) % tiles_per_seq == 0)
        def _():
            s_ref[0:CONV_PAD, :] = jnp.zeros((CONV_PAD, CONV_WIDTH), F32)
    else:
        s_ref[0:CONV_PAD, :] = jnp.zeros((CONV_PAD, CONV_WIDTH), F32)
    s_ref[CONV_PAD:CONV_PAD + tm, :] = u
    prev1 = s_ref[CONV_PAD - 1:CONV_PAD - 1 + tm, :]
    prev2 = s_ref[CONV_PAD - 2:CONV_PAD - 2 + tm, :]
    if prompt:
        s_ref[CONV_PAD - 2:CONV_PAD, :] = u[tm - 2:tm, :]
        cstate[0] = u[tm - 2:tm, :]
    else:
        r = lax.broadcasted_iota(jnp.int32, (tm, 1), 0) % seq_in_tile
        prev1 = jnp.where(r < 1, e1_ref[...], prev1)
        prev2 = jnp.where(r < 2, e2_ref[...], prev2)
        cstate[...] = u
    conv = cw_ref[0:1, :] * prev2 + cw_ref[1:2, :] * prev1 + cw_ref[2:3, :] * u
    cpre = (cb * conv).astype(BF16)

    sga[...] = jax.nn.sigmoid(proj(o_ga, D_MODEL)).astype(BF16)
    gyc[...] = (jax.nn.sigmoid(proj(o_gc, D_MODEL)) * _dot(cpre, wc_ref[...])).astype(BF16)

    k = proj(o_k, SB_WIDTH)
    v = proj(o_v, SB_WIDTH)
    if prompt:
        kt = k.T
        k_out[0] = kt
        for j in range(tm // KEY_TILE):
            k_bf[0, j] = kt[:, j * KEY_TILE:(j + 1) * KEY_TILE].astype(BF16)
        v_out[0] = v.T
    else:
        k_out[...] = k
        k_bf[...] = k.astype(BF16)
        v_out[...] = v
    v_bf[...] = v.astype(BF16)
    q_bf[...] = (proj(o_q, SB_WIDTH) * (LOG2E / math.sqrt(HEAD_DIM))).astype(BF16)


def _proj_call(x, g, w_in, conv_w, w_conv_out, seq_len, hist=None):
    n = x.shape[0]
    tm = min(ROW_TILE, n)
    grid = (n // tm,)
    sample = hist is not None
    sds = jax.ShapeDtypeStruct
    row = lambda w: pl.BlockSpec((tm, w), lambda t: (t, 0))
    in_specs = [row(D_MODEL), _resident((1, D_MODEL)), _resident((D_MODEL, N_PROJ)),
                _resident((CONV_K, CONV_WIDTH)), _resident((CONV_WIDTH, D_MODEL))]
    args = [x, g.reshape(1, D_MODEL), w_in, conv_w, w_conv_out]
    if sample:
        in_specs += [row(CONV_WIDTH), row(CONV_WIDTH)]
        args += list(hist)
        tiles_per_seq, seq_in_tile = None, seq_len
        kv_shape = sds((n, SB_WIDTH), F32)
        kv_spec = row(SB_WIDTH)
        kbf_shape = sds((n, SB_WIDTH), BF16)
        kbf_spec = row(SB_WIDTH)
        cstate_shape = sds((n, CONV_WIDTH), F32)
        cstate_spec = row(CONV_WIDTH)
    else:
        tps = seq_len // tm
        nb = tm // KEY_TILE
        tiles_per_seq, seq_in_tile = tps, None
        batch = n // seq_len
        kv_shape = sds((batch, SB_WIDTH, seq_len), F32)
        kv_spec = pl.BlockSpec((1, SB_WIDTH, tm), lambda t: (t // tps, 0, t % tps))
        kbf_shape = sds((batch, seq_len // KEY_TILE, SB_WIDTH, KEY_TILE), BF16)
        kbf_spec = pl.BlockSpec((1, nb, SB_WIDTH, KEY_TILE), lambda t: (t // tps, t % tps, 0, 0))
        cstate_shape = sds((batch, CONV_K - 1, CONV_WIDTH), F32)
        cstate_spec = pl.BlockSpec((1, CONV_K - 1, CONV_WIDTH), lambda t: (t // tps, 0, 0))
    out_shape = (kv_shape, kv_shape,
                 sds((n, SB_WIDTH), BF16), kbf_shape, sds((n, SB_WIDTH), BF16),
                 sds((n, D_MODEL), BF16), sds((n, D_MODEL), BF16), cstate_shape)
    out_specs = (kv_spec, kv_spec, row(SB_WIDTH), kbf_spec, row(SB_WIDTH),
                 row(D_MODEL), row(D_MODEL), cstate_spec)
    return pl.pallas_call(
        functools.partial(_proj_kernel, tiles_per_seq=tiles_per_seq, seq_in_tile=seq_in_tile),
        grid=grid, in_specs=in_specs, out_specs=out_specs, out_shape=out_shape,
        scratch_shapes=[pltpu.VMEM((tm + CONV_PAD, CONV_WIDTH), F32)],
        compiler_params=pltpu.CompilerParams(
            dimension_semantics=("arbitrary",), vmem_limit_bytes=VMEM_LIMIT),
        name="proj_sample" if sample else "proj_prompt",
    )(*args)


def _softplus2(z):
    return jnp.maximum(z, 0.0) + jnp.log(1.0 + jnp.exp2(-jnp.abs(z))) * LOG2E


def _finish_block(z, vw, tri, carry, v_is_t=False):
    tail = _dot(_softplus2(z).astype(BF16), tri)
    a = jnp.exp2(z - tail + carry).astype(BF16)
    pv = _dot_nt(a, vw) if v_is_t else _dot(a, vw)
    return pv, carry - tail[:, 0:1]


def _stack_heads(qp):
    lane = lax.broadcasted_iota(jnp.int32, (1, LANES), 1)
    zero = jnp.zeros((), BF16)
    return jnp.concatenate([jnp.where(lane < HEAD_DIM, qp, zero),
                            jnp.where(lane >= HEAD_DIM, qp, zero)], axis=0)


def _unstack_heads(acc):
    lane = lax.broadcasted_iota(jnp.int32, (1, LANES), 1)
    half = acc.shape[0] // 2
    return jnp.where(lane < HEAD_DIM, acc[:half], acc[half:])


def _pair(p):
    return slice(p * LANES, (p + 1) * LANES)


def _attn_prompt_kernel(q_ref, kt_ref, v_ref, tri_ref, bias_ref, o_ref, carry_ref, acc_ref):
    i = pl.program_id(1)
    n_sub = q_ref.shape[0] // KEY_TILE
    rows = HEADS_PER_TILE * KEY_TILE
    causal, left_off = bias_ref[0], bias_ref[1]
    inst = [(p, s) for p in range(N_PAIRS) for s in range(n_sub)]

    def q2_of(p, s):
        return _stack_heads(q_ref[s * KEY_TILE:(s + 1) * KEY_TILE, _pair(p)])

    def window(p, jr):
        jl = jnp.maximum(jr - 1, 0)
        kw = jnp.concatenate([kt_ref[0, jl, _pair(p), :], kt_ref[0, jr, _pair(p), :]], axis=1)
        rows_of = lambda j: pl.ds(pl.multiple_of(j * KEY_TILE, KEY_TILE), KEY_TILE)
        vw = jnp.concatenate([v_ref[rows_of(jl), _pair(p)], v_ref[rows_of(jr), _pair(p)]], axis=0)
        return kw, vw

    first_bias = jnp.where(i == 0, causal + left_off, causal)
    zs, vws = [], []
    for p, s in inst:
        kw, vw = window(p, n_sub * i + s)
        zs.append(_dot(q2_of(p, s), kw) + (first_bias if s == 0 else causal))
        vws.append(vw)
    sps = jnp.concatenate([_softplus2(z).astype(BF16) for z in zs], axis=0)
    tails = _dot(sps, tri_ref[...])
    mx = []
    for n, (p, s) in enumerate(inst):
        tail = tails[n * rows:(n + 1) * rows]
        a = jnp.exp2(zs[n] - tail).astype(BF16)
        acc_ref[p, s] = _dot(a, vws[n])
        carry = -tail[:, 0:1]
        carry_ref[p, s] = carry
        mx.append(jnp.max(carry))

    for n, (p, s) in enumerate(inst):
        def body(state, p=p, s=s):
            jr, _ = state
            kw, vw = window(p, jr)
            z = _dot(q2_of(p, s), kw) + jnp.where(jr >= 1, jnp.zeros_like(left_off), left_off)
            pv, carry = _finish_block(z, vw, tri_ref[...], carry_ref[p, s])
            acc_ref[p, s] += pv
            carry_ref[p, s] = carry
            return jr - 2, jnp.max(carry)

        lax.while_loop(lambda st: (st[0] >= 0) & (st[1] > LOG2_ZERO), body,
                       (n_sub * i + s - 2, mx[n]))

    for p, s in inst:
        o_ref[s * KEY_TILE:(s + 1) * KEY_TILE, _pair(p)] = _unstack_heads(acc_ref[p, s]).astype(BF16)


def _attn_prompt_call(q_bf, kt_bf, v_bf, tri, bias, batch, seq):
    tq = ATT_ROWS
    nq = seq // tq
    n_sub = tq // KEY_TILE
    rows = HEADS_PER_TILE * KEY_TILE
    return pl.pallas_call(
        _attn_prompt_kernel,
        grid=(batch, nq),
        in_specs=[pl.BlockSpec((tq, SB_WIDTH), lambda b, i: (b * nq + i, 0)),
                  pl.BlockSpec((1, seq // KEY_TILE, SB_WIDTH, KEY_TILE), lambda b, i: (b, 0, 0, 0)),
                  pl.BlockSpec((seq, SB_WIDTH), lambda b, i: (b, 0)),
                  _resident(tri.shape), _resident(bias.shape)],
        out_specs=pl.BlockSpec((tq, SB_WIDTH), lambda b, i: (b * nq + i, 0)),
        out_shape=jax.ShapeDtypeStruct((batch * seq, SB_WIDTH), BF16),
        scratch_shapes=[pltpu.VMEM((N_PAIRS, n_sub, rows, 1), F32),
                        pltpu.VMEM((N_PAIRS, n_sub, rows, LANES), F32)],
        compiler_params=pltpu.CompilerParams(
            dimension_semantics=("arbitrary", "arbitrary"), vmem_limit_bytes=VMEM_LIMIT),
        name="attn_prompt",
    )(q_bf, kt_bf, v_bf, tri, bias)


def _attn_sample_kernel(q_ref, kn_ref, vn_ref, ck0_ref, cv0_ref, ck_any, cv_any, tri_ref, bias_ref,
                        o_ref, carry_ref, acc_ref, kpad_ref, vpad_ref, kbuf_ref, vbuf_ref, sem):
    b = pl.program_id(0)
    tq = q_ref.shape[0]
    tk = tri_ref.shape[0]
    n_win = ck_any.shape[2] // tk
    rows = HEADS_PER_TILE * tq
    kpad_ref[...] = jnp.zeros_like(kpad_ref)
    vpad_ref[...] = jnp.zeros_like(vpad_ref)
    kpad_ref[0:tq, :] = kn_ref[...]
    vpad_ref[0:tq, :] = vn_ref[...]
    zs = []
    for p in range(N_PAIRS):
        q2 = _stack_heads(q_ref[:, _pair(p)])
        zs.append(_dot_nt(q2, kpad_ref[:, _pair(p)]) + bias_ref[...])
        zs.append(_dot(q2, ck0_ref[0, _pair(p), :].astype(BF16)))
    tails = _dot(jnp.concatenate([_softplus2(z).astype(BF16) for z in zs], axis=0), tri_ref[...])
    mx = []
    for p in range(N_PAIRS):
        t_new = tails[(2 * p) * rows:(2 * p + 1) * rows]
        t_old = tails[(2 * p + 1) * rows:(2 * p + 2) * rows]
        c_new = -t_new[:, 0:1]
        a_new = jnp.exp2(zs[2 * p] - t_new).astype(BF16)
        a_old = jnp.exp2(zs[2 * p + 1] - t_old + c_new).astype(BF16)
        acc_ref[p] = (_dot(a_new, vpad_ref[:, _pair(p)])
                      + _dot_nt(a_old, cv0_ref[0, _pair(p), :].astype(BF16)))
        carry = c_new - t_old[:, 0:1]
        carry_ref[p] = carry
        mx.append(jnp.max(carry))
    for p in range(N_PAIRS):
        def window(src, dst, j, slot, p=p):
            k0 = pl.multiple_of(j * tk, tk)
            return pltpu.make_async_copy(src.at[b, _pair(p), pl.ds(k0, tk)], dst, sem.at[slot])

        def body(state, p=p, window=window):
            j, _ = state
            window(ck_any, kbuf_ref, j, 0).start()
            window(cv_any, vbuf_ref, j, 1).start()
            window(ck_any, kbuf_ref, j, 0).wait()
            window(cv_any, vbuf_ref, j, 1).wait()
            z = _dot(_stack_heads(q_ref[:, _pair(p)]), kbuf_ref[...].astype(BF16))
            pv, carry = _finish_block(z, vbuf_ref[...].astype(BF16), tri_ref[...], carry_ref[p], True)
            acc_ref[p] += pv
            carry_ref[p] = carry
            return j - 1, jnp.max(carry)

        lax.while_loop(lambda s: (s[0] >= 0) & (s[1] > LOG2_ZERO), body, (n_win - 2, mx[p]))
    for p in range(N_PAIRS):
        o_ref[:, _pair(p)] = _unstack_heads(acc_ref[p]).astype(BF16)


def _attn_sample_call(q_bf, k_bf, v_bf, ckt, cvt, tri, bias, batch, seq):
    past = ckt.shape[2]
    tk = tri.shape[0]
    n_win = past // tk
    row = pl.BlockSpec((seq, SB_WIDTH), lambda b: (b, 0))
    last = pl.BlockSpec((1, SB_WIDTH, tk), lambda b: (b, 0, n_win - 1))
    hbm = pl.BlockSpec(memory_space=pl.ANY)
    return pl.pallas_call(
        _attn_sample_kernel,
        grid=(batch,),
        in_specs=[row, row, row, last, last, hbm, hbm, _resident(tri.shape), _resident(bias.shape)],
        out_specs=row,
        out_shape=jax.ShapeDtypeStruct((batch * seq, SB_WIDTH), BF16),
        scratch_shapes=[pltpu.VMEM((N_PAIRS, 2 * seq, 1), F32),
                        pltpu.VMEM((N_PAIRS, 2 * seq, LANES), F32),
                        pltpu.VMEM((tk, SB_WIDTH), BF16), pltpu.VMEM((tk, SB_WIDTH), BF16),
                        pltpu.VMEM((LANES, tk), F32), pltpu.VMEM((LANES, tk), F32),
                        pltpu.SemaphoreType.DMA((2,))],
        compiler_params=pltpu.CompilerParams(
            dimension_semantics=("arbitrary",), vmem_limit_bytes=VMEM_LIMIT),
        name="attn_sample",
    )(q_bf, k_bf, v_bf, ckt, cvt, ckt, cvt, tri, bias)


def _post_kernel(x_ref, att_ref, sga_ref, gyc_ref, p_ref,
                 wa_ref, wo_ref, gf_ref, wu_ref, wd_ref, gp_ref, wg_ref, wp_ref, gl_ref,
                 y_ref):
    tm = x_ref.shape[0]
    groups = [slice(r0, r0 + tm // POST_GROUPS) for r0 in range(0, tm, tm // POST_GROUPS)]
    each = lambda fn, *cols: [fn(*vals) for vals in zip(*cols)]
    ya = each(lambda rs: _dot(att_ref[rs], wa_ref[...]), groups)
    merged = each(lambda rs, ya: sga_ref[rs].astype(F32) * ya + gyc_ref[rs].astype(F32), groups, ya)
    x1 = each(lambda rs, m: x_ref[rs] + _dot(m.astype(BF16), wo_ref[...]), groups, merged)
    h2 = each(lambda x: _rms(x, gf_ref[...]).astype(BF16), x1)
    f = each(jnp.zeros_like, x1)
    for c in range(0, D_FF, FF_CHUNK):
        up = each(lambda h: jnp.maximum(_dot(h, wu_ref[:, c:c + FF_CHUNK]), 0.0), h2)
        f = each(lambda f, u: f + _dot((u * u).astype(BF16), wd_ref[c:c + FF_CHUNK, :]), f, up)
    x2 = each(lambda a, b: a + b, x1, f)
    h3 = each(lambda x: _rms(x, gp_ref[...]).astype(BF16), x2)
    gate = each(lambda h: jax.nn.sigmoid(_dot(h, wg_ref[...])), h3)
    x3 = each(lambda rs, x, g: x + g * _dot(p_ref[rs].astype(BF16), wp_ref[...]), groups, x2, gate)
    for rs, x in zip(groups, x3):
        y_ref[rs] = _rms(x, gl_ref[...])


def _post_call(x, att, sga, gyc, p, wa, wo, gf, wu, wd, gp, wg, wp, gl, name):
    n = x.shape[0]
    tm = min(ROW_TILE, n)
    row = lambda w: pl.BlockSpec((tm, w), lambda t: (t, 0))
    vec = lambda g: g.reshape(1, D_MODEL)
    return pl.pallas_call(
        _post_kernel,
        grid=(n // tm,),
        in_specs=[row(D_MODEL), row(SB_WIDTH), row(D_MODEL), row(D_MODEL), row(PLE_DIM),
                  _resident(wa.shape), _resident(wo.shape),
                  _resident((1, D_MODEL)), _resident(wu.shape), _resident(wd.shape),
                  _resident((1, D_MODEL)), _resident(wg.shape), _resident(wp.shape),
                  _resident((1, D_MODEL))],
        out_specs=row(D_MODEL),
        out_shape=jax.ShapeDtypeStruct((n, D_MODEL), F32),
        compiler_params=pltpu.CompilerParams(
            dimension_semantics=("arbitrary",), vmem_limit_bytes=VMEM_LIMIT),
        name=name,
    )(x, att, sga, gyc, p, wa, wo, vec(gf), wu, wd, vec(gp), wg, wp, vec(gl))


def _causal_bias(rows, cols, period, first_query=0):
    r = lax.broadcasted_iota(jnp.int32, (rows, cols), 0) % period + first_query
    c = lax.broadcasted_iota(jnp.int32, (rows, cols), 1)
    return jnp.where(c < r, 0.0, -jnp.inf).astype(F32)


def _prompt_bias():
    rows, cols = HEADS_PER_TILE * KEY_TILE, 2 * KEY_TILE
    c = lax.broadcasted_iota(jnp.int32, (rows, cols), 1)
    left_off = jnp.where(c < KEY_TILE, -jnp.inf, 0.0).astype(F32)
    return jnp.stack([_causal_bias(rows, cols, KEY_TILE, KEY_TILE), left_off])


def kernel(x_prompt, x_sample, p_prompt, p_sample, cache_k, cache_v, cache_conv,
           g_mix, w_in, conv_w, w_attn_out, w_conv_out, w_o,
           g_ffn, w_up, w_down, g_ple, w_ple_gate, w_ple, g_final):
    depth = w_in.shape[0]
    assert depth == 1
    b, t, _ = x_prompt.shape
    db, dt, _ = x_sample.shape
    past = cache_k.shape[2]
    bf = lambda w: w[0].astype(BF16)
    w_in_b, wa, wc, wo = bf(w_in), bf(w_attn_out), bf(w_conv_out), bf(w_o)
    wu, wd, wg, wp = bf(w_up), bf(w_down), bf(w_ple_gate), bf(w_ple)
    r = lax.broadcasted_iota(jnp.int32, (ATT_BLOCK, ATT_BLOCK), 0)
    c = lax.broadcasted_iota(jnp.int32, (ATT_BLOCK, ATT_BLOCK), 1)
    tri = (r >= c).astype(BF16)
    hd = (N_HEADS, HEAD_DIM)

    xp = x_prompt.reshape(b * t, D_MODEL)
    kpt, vpt, q_bf, kt_bf, v_bf, sga, gyc, conv_p = _proj_call(
        xp, g_mix[0], w_in_b, conv_w[0], wc, t)
    att = _attn_prompt_call(q_bf, kt_bf, v_bf, tri, _prompt_bias(), b, t)
    yp = _post_call(xp, att, sga, gyc, p_prompt[0].reshape(b * t, PLE_DIM),
                    wa, wo, g_ffn[0], wu, wd, g_ple[0], wg, wp, g_final, "post_prompt")
    kp = kpt.reshape(b, *hd, t).transpose(0, 3, 1, 2)
    vp = vpt.reshape(b, *hd, t).transpose(0, 3, 1, 2)

    xs = x_sample.reshape(db * dt, D_MODEL)
    buf = cache_conv[0]
    zeros = jnp.zeros((db, dt, CONV_WIDTH), F32)
    e1 = zeros.at[:, 0].set(buf[:, 1]).reshape(db * dt, CONV_WIDTH)
    e2 = zeros.at[:, 0].set(buf[:, 0]).at[:, 1].set(buf[:, 1]).reshape(db * dt, CONV_WIDTH)
    ks, vs, q_bf, k_bf, v_bf, sga, gyc, u_s = _proj_call(
        xs, g_mix[0], w_in_b, conv_w[0], wc, dt, hist=(e1, e2))
    ckt = cache_k[0].transpose(0, 2, 3, 1).reshape(db, SB_WIDTH, past)
    cvt = cache_v[0].transpose(0, 2, 3, 1).reshape(db, SB_WIDTH, past)
    att = _attn_sample_call(q_bf, k_bf, v_bf, ckt, cvt, tri,
                            _causal_bias(HEADS_PER_TILE * dt, ATT_BLOCK, dt), db, dt)
    ys = _post_call(xs, att, sga, gyc, p_sample[0].reshape(db * dt, PLE_DIM),
                    wa, wo, g_ffn[0], wu, wd, g_ple[0], wg, wp, g_final, "post_sample")
    conv_s = u_s.reshape(db, dt, CONV_WIDTH)[:, dt - (CONV_K - 1):]

    return (yp.reshape(b, t, D_MODEL), ys.reshape(db, dt, D_MODEL),
            kp[None], vp[None], conv_p[None],
            ks.reshape(1, db, dt, *hd), vs.reshape(1, db, dt, *hd), conv_s[None])
```

```python
import functools
import math

import jax
import jax.numpy as jnp
from jax import lax
from jax.experimental import pallas as pl
from jax.experimental.pallas import tpu as pltpu

F32 = jnp.float32
BF16 = jnp.bfloat16

D_MODEL = 1024
N_HEADS = 8
HEAD_DIM = 64
SB_WIDTH = N_HEADS * HEAD_DIM
CONV_WIDTH = 512
CONV_K = 3
PLE_DIM = 256
D_FF = 4 * D_MODEL
EPS = 1e-6
N_PROJ = 3 * SB_WIDTH + 3 * CONV_WIDTH + 2 * D_MODEL

LANES = 128
HEADS_PER_TILE = LANES // HEAD_DIM
N_PAIRS = N_HEADS // HEADS_PER_TILE
PROJ_ROWS = 1024
POST_ROWS = 512
ATT_BLOCK = 256
ATT_ROWS = 512
FF_CHUNK = 1024
POST_GROUPS = 2
CONV_PAD = 8
LOG2E = 1.4426950408889634
LOG2_ZERO = -126.0
KEY_TILE = LANES
VMEM_LIMIT = 52 * 1024 * 1024


def _resident(shape):
    nd = len(shape)
    return pl.BlockSpec(shape, lambda *_: (0,) * nd, pipeline_mode=pl.Buffered(1))


def _rms(x, g):
    return (x * lax.rsqrt(jnp.mean(x * x, axis=-1, keepdims=True) + EPS)) * g


def _dot(a, b):
    return jnp.dot(a, b, preferred_element_type=F32)


def _dot_nt(a, b):
    return lax.dot_general(a, b, (((1,), (1,)), ((), ())), preferred_element_type=F32)


def _proj_kernel(*refs, tiles_per_seq, seq_in_tile):
    prompt = seq_in_tile is None
    if prompt:
        (x_ref, g_ref, w_ref, cw_ref, wc_ref,
         k_out, v_out, q_bf, k_bf, v_bf, sga, gyc, cstate, s_ref) = refs
    else:
        (x_ref, g_ref, w_ref, cw_ref, wc_ref, e1_ref, e2_ref,
         k_out, v_out, q_bf, k_bf, v_bf, sga, gyc, cstate, s_ref) = refs
    tm = x_ref.shape[0]
    hb = _rms(x_ref[...], g_ref[...]).astype(BF16)
    o_q, o_k, o_v = 0, SB_WIDTH, 2 * SB_WIDTH
    o_cb = 3 * SB_WIDTH
    o_cc, o_cx = o_cb + CONV_WIDTH, o_cb + 2 * CONV_WIDTH
    o_ga = o_cb + 3 * CONV_WIDTH
    o_gc = o_ga + D_MODEL

    def proj(lo, width):
        return _dot(hb, w_ref[:, lo:lo + width])

    cb = proj(o_cb, CONV_WIDTH)
    u = proj(o_cc, CONV_WIDTH) * proj(o_cx, CONV_WIDTH)
    if prompt:
        @pl.when(pl.program_id(0) % tiles_per_seq == 0)
        def _():
            s_ref[0:CONV_PAD, :] = jnp.zeros((CONV_PAD, CONV_WIDTH), F32)
    else:
        s_ref[0:CONV_PAD, :] = jnp.zeros((CONV_PAD, CONV_WIDTH), F32)
    s_ref[CONV_PAD:CONV_PAD + tm, :] = u
    prev1 = s_ref[CONV_PAD - 1:CONV_PAD - 1 + tm, :]
    prev2 = s_ref[CONV_PAD - 2:CONV_PAD - 2 + tm, :]
    if prompt:
        s_ref[CONV_PAD - 2:CONV_PAD, :] = u[tm - 2:tm, :]
        cstate[0] = u[tm - 2:tm, :]
    else:
        r = lax.broadcasted_iota(jnp.int32, (tm, 1), 0) % seq_in_tile
        prev1 = jnp.where(r < 1, e1_ref[...], prev1)
        prev2 = jnp.where(r < 2, e2_ref[...], prev2)
        cstate[...] = u
    conv = cw_ref[0:1, :] * prev2 + cw_ref[1:2, :] * prev1 + cw_ref[2:3, :] * u
    cpre = (cb * conv).astype(BF16)

    sga[...] = jax.nn.sigmoid(proj(o_ga, D_MODEL)).astype(BF16)
    gyc[...] = (jax.nn.sigmoid(proj(o_gc, D_MODEL)) * _dot(cpre, wc_ref[...])).astype(BF16)

    k = proj(o_k, SB_WIDTH)
    v = proj(o_v, SB_WIDTH)
    if prompt:
        kt = k.T
        k_out[0] = kt
        for j in range(tm // KEY_TILE):
            k_bf[0, j] = kt[:, j * KEY_TILE:(j + 1) * KEY_TILE].astype(BF16)
        v_out[0] = v.T
    else:
        k_out[...] = k
        k_bf[...] = k.astype(BF16)
        v_out[...] = v
    v_bf[...] = v.astype(BF16)
    q_bf[...] = (proj(o_q, SB_WIDTH) * (LOG2E / math.sqrt(HEAD_DIM))).astype(BF16)


def _proj_call(x, g, w_in, conv_w, w_conv_out, seq_len, hist=None):
    n = x.shape[0]
    tm = min(PROJ_ROWS, n)
    grid = (n // tm,)
    sample = hist is not None
    sds = jax.ShapeDtypeStruct
    row = lambda w: pl.BlockSpec((tm, w), lambda t: (t, 0))
    in_specs = [row(D_MODEL), _resident((1, D_MODEL)), _resident((D_MODEL, N_PROJ)),
                _resident((CONV_K, CONV_WIDTH)), _resident((CONV_WIDTH, D_MODEL))]
    args = [x, g.reshape(1, D_MODEL), w_in, conv_w, w_conv_out]
    if sample:
        in_specs += [row(CONV_WIDTH), row(CONV_WIDTH)]
        args += list(hist)
        tiles_per_seq, seq_in_tile = None, seq_len
        kv_shape = sds((n, SB_WIDTH), F32)
        kv_spec = row(SB_WIDTH)
        kbf_shape = sds((n, SB_WIDTH), BF16)
        kbf_spec = row(SB_WIDTH)
        cstate_shape = sds((n, CONV_WIDTH), F32)
        cstate_spec = row(CONV_WIDTH)
    else:
        tps = seq_len // tm
        nb = tm // KEY_TILE
        tiles_per_seq, seq_in_tile = tps, None
        batch = n // seq_len
        kv_shape = sds((batch, SB_WIDTH, seq_len), F32)
        kv_spec = pl.BlockSpec((1, SB_WIDTH, tm), lambda t: (t // tps, 0, t % tps))
        kbf_shape = sds((batch, seq_len // KEY_TILE, SB_WIDTH, KEY_TILE), BF16)
        kbf_spec = pl.BlockSpec((1, nb, SB_WIDTH, KEY_TILE), lambda t: (t // tps, t % tps, 0, 0))
        cstate_shape = sds((batch, CONV_K - 1, CONV_WIDTH), F32)
        cstate_spec = pl.BlockSpec((1, CONV_K - 1, CONV_WIDTH), lambda t: (t // tps, 0, 0))
    out_shape = (kv_shape, kv_shape,
                 sds((n, SB_WIDTH), BF16), kbf_shape, sds((n, SB_WIDTH), BF16),
                 sds((n, D_MODEL), BF16), sds((n, D_MODEL), BF16), cstate_shape)
    out_specs = (kv_spec, kv_spec, row(SB_WIDTH), kbf_spec, row(SB_WIDTH),
                 row(D_MODEL), row(D_MODEL), cstate_spec)
    return pl.pallas_call(
        functools.partial(_proj_kernel, tiles_per_seq=tiles_per_seq, seq_in_tile=seq_in_tile),
        grid=grid, in_specs=in_specs, out_specs=out_specs, out_shape=out_shape,
        scratch_shapes=[pltpu.VMEM((tm + CONV_PAD, CONV_WIDTH), F32)],
        compiler_params=pltpu.CompilerParams(
            dimension_semantics=("arbitrary",), vmem_limit_bytes=VMEM_LIMIT),
        name="proj_sample" if sample else "proj_prompt",
    )(*args)


def _softplus2(z):
    return jnp.maximum(z, 0.0) + jnp.log(1.0 + jnp.exp2(-jnp.abs(z))) * LOG2E


def _finish_block(z, vw, tri, carry, v_is_t=False):
    tail = _dot(_softplus2(z).astype(BF16), tri)
    a = jnp.exp2(z - tail + carry).astype(BF16)
    pv = _dot_nt(a, vw) if v_is_t else _dot(a, vw)
    return pv, carry - tail[:, 0:1]


def _stack_heads(qp):
    lane = lax.broadcasted_iota(jnp.int32, (1, LANES), 1)
    zero = jnp.zeros((), BF16)
    return jnp.concatenate([jnp.where(lane < HEAD_DIM, qp, zero),
                            jnp.where(lane >= HEAD_DIM, qp, zero)], axis=0)


def _unstack_heads(acc):
    lane = lax.broadcasted_iota(jnp.int32, (1, LANES), 1)
    half = acc.shape[0] // 2
    return jnp.where(lane < HEAD_DIM, acc[:half], acc[half:])


def _pair(p):
    return slice(p * LANES, (p + 1) * LANES)


def _attn_prompt_kernel(q_ref, kt_ref, v_ref, tri_ref, bias_ref, o_ref, carry_ref, acc_ref):
    i = pl.program_id(1)
    n_sub = q_ref.shape[0] // KEY_TILE
    rows = HEADS_PER_TILE * KEY_TILE
    causal, left_off = bias_ref[0], bias_ref[1]
    inst = [(p, s) for p in range(N_PAIRS) for s in range(n_sub)]

    def q2_of(p, s):
        return _stack_heads(q_ref[s * KEY_TILE:(s + 1) * KEY_TILE, _pair(p)])

    def window(p, jr):
        jl = jnp.maximum(jr - 1, 0)
        kw = jnp.concatenate([kt_ref[0, jl, _pair(p), :], kt_ref[0, jr, _pair(p), :]], axis=1)
        rows_of = lambda j: pl.ds(pl.multiple_of(j * KEY_TILE, KEY_TILE), KEY_TILE)
        vw = jnp.concatenate([v_ref[rows_of(jl), _pair(p)], v_ref[rows_of(jr), _pair(p)]], axis=0)
        return kw, vw

    first_bias = jnp.where(i == 0, causal + left_off, causal)
    zs, vws = [], []
    for p, s in inst:
        kw, vw = window(p, n_sub * i + s)
        zs.append(_dot(q2_of(p, s), kw) + (first_bias if s == 0 else causal))
        vws.append(vw)
    sps = jnp.concatenate([_softplus2(z).astype(BF16) for z in zs], axis=0)
    tails = _dot(sps, tri_ref[...])
    mx = []
    for n, (p, s) in enumerate(inst):
        tail = tails[n * rows:(n + 1) * rows]
        a = jnp.exp2(zs[n] - tail).astype(BF16)
        acc_ref[p, s] = _dot(a, vws[n])
        carry = -tail[:, 0:1]
        carry_ref[p, s] = carry
        mx.append(jnp.max(carry))

    for n, (p, s) in enumerate(inst):
        def body(state, p=p, s=s):
            jr, _ = state
            kw, vw = window(p, jr)
            z = _dot(q2_of(p, s), kw) + jnp.where(jr >= 1, jnp.zeros_like(left_off), left_off)
            pv, carry = _finish_block(z, vw, tri_ref[...], carry_ref[p, s])
            acc_ref[p, s] += pv
            carry_ref[p, s] = carry
            return jr - 2, jnp.max(carry)

        lax.while_loop(lambda st: (st[0] >= 0) & (st[1] > LOG2_ZERO), body,
                       (n_sub * i + s - 2, mx[n]))

    for p, s in inst:
        o_ref[s * KEY_TILE:(s + 1) * KEY_TILE, _pair(p)] = _unstack_heads(acc_ref[p, s]).astype(BF16)


def _attn_prompt_call(q_bf, kt_bf, v_bf, tri, bias, batch, seq):
    tq = ATT_ROWS
    nq = seq // tq
    n_sub = tq // KEY_TILE
    rows = HEADS_PER_TILE * KEY_TILE
    return pl.pallas_call(
        _attn_prompt_kernel,
        grid=(batch, nq),
        in_specs=[pl.BlockSpec((tq, SB_WIDTH), lambda b, i: (b * nq + i, 0)),
                  pl.BlockSpec((1, seq // KEY_TILE, SB_WIDTH, KEY_TILE), lambda b, i: (b, 0, 0, 0)),
                  pl.BlockSpec((seq, SB_WIDTH), lambda b, i: (b, 0)),
                  _resident(tri.shape), _resident(bias.shape)],
        out_specs=pl.BlockSpec((tq, SB_WIDTH), lambda b, i: (b * nq + i, 0)),
        out_shape=jax.ShapeDtypeStruct((batch * seq, SB_WIDTH), BF16),
        scratch_shapes=[pltpu.VMEM((N_PAIRS, n_sub, rows, 1), F32),
                        pltpu.VMEM((N_PAIRS, n_sub, rows, LANES), F32)],
        compiler_params=pltpu.CompilerParams(
            dimension_semantics=("arbitrary", "arbitrary"), vmem_limit_bytes=VMEM_LIMIT),
        name="attn_prompt",
    )(q_bf, kt_bf, v_bf, tri, bias)


def _attn_sample_kernel(q_ref, kn_ref, vn_ref, ck0_ref, cv0_ref, ck_any, cv_any, tri_ref, bias_ref,
                        o_ref, carry_ref, acc_ref, kpad_ref, vpad_ref, kbuf_ref, vbuf_ref, sem):
    b = pl.program_id(0)
    tq = q_ref.shape[0]
    tk = tri_ref.shape[0]
    n_win = ck_any.shape[2] // tk
    rows = HEADS_PER_TILE * tq
    kpad_ref[...] = jnp.zeros_like(kpad_ref)
    vpad_ref[...] = jnp.zeros_like(vpad_ref)
    kpad_ref[0:tq, :] = kn_ref[...]
    vpad_ref[0:tq, :] = vn_ref[...]
    zs = []
    for p in range(N_PAIRS):
        q2 = _stack_heads(q_ref[:, _pair(p)])
        zs.append(_dot_nt(q2, kpad_ref[:, _pair(p)]) + bias_ref[...])
        zs.append(_dot(q2, ck0_ref[0, _pair(p), :].astype(BF16)))
    tails = _dot(jnp.concatenate([_softplus2(z).astype(BF16) for z in zs], axis=0), tri_ref[...])
    mx = []
    for p in range(N_PAIRS):
        t_new = tails[(2 * p) * rows:(2 * p + 1) * rows]
        t_old = tails[(2 * p + 1) * rows:(2 * p + 2) * rows]
        c_new = -t_new[:, 0:1]
        a_new = jnp.exp2(zs[2 * p] - t_new).astype(BF16)
        a_old = jnp.exp2(zs[2 * p + 1] - t_old + c_new).astype(BF16)
        acc_ref[p] = (_dot(a_new, vpad_ref[:, _pair(p)])
                      + _dot_nt(a_old, cv0_ref[0, _pair(p), :].astype(BF16)))
        carry = c_new - t_old[:, 0:1]
        carry_ref[p] = carry
        mx.append(jnp.max(carry))
    for p in range(N_PAIRS):
        def window(src, dst, j, slot, p=p):
            k0 = pl.multiple_of(j * tk, tk)
            return pltpu.make_async_copy(src.at[b, _pair(p), pl.ds(k0, tk)], dst, sem.at[slot])

        def body(state, p=p, window=window):
            j, _ = state
            window(ck_any, kbuf_ref, j, 0).start()
            window(cv_any, vbuf_ref, j, 1).start()
            window(ck_any, kbuf_ref, j, 0).wait()
            window(cv_any, vbuf_ref, j, 1).wait()
            z = _dot(_stack_heads(q_ref[:, _pair(p)]), kbuf_ref[...].astype(BF16))
            pv, carry = _finish_block(z, vbuf_ref[...].astype(BF16), tri_ref[...], carry_ref[p], True)
            acc_ref[p] += pv
            carry_ref[p] = carry
            return j - 1, jnp.max(carry)

        lax.while_loop(lambda s: (s[0] >= 0) & (s[1] > LOG2_ZERO), body, (n_win - 2, mx[p]))
    for p in range(N_PAIRS):
        o_ref[:, _pair(p)] = _unstack_heads(acc_ref[p]).astype(BF16)


def _attn_sample_call(q_bf, k_bf, v_bf, ckt, cvt, tri, bias, batch, seq):
    past = ckt.shape[2]
    tk = tri.shape[0]
    n_win = past // tk
    row = pl.BlockSpec((seq, SB_WIDTH), lambda b: (b, 0))
    last = pl.BlockSpec((1, SB_WIDTH, tk), lambda b: (b, 0, n_win - 1))
    hbm = pl.BlockSpec(memory_space=pl.ANY)
    return pl.pallas_call(
        _attn_sample_kernel,
        grid=(batch,),
        in_specs=[row, row, row, last, last, hbm, hbm, _resident(tri.shape), _resident(bias.shape)],
        out_specs=row,
        out_shape=jax.ShapeDtypeStruct((batch * seq, SB_WIDTH), BF16),
        scratch_shapes=[pltpu.VMEM((N_PAIRS, 2 * seq, 1), F32),
                        pltpu.VMEM((N_PAIRS, 2 * seq, LANES), F32),
                        pltpu.VMEM((tk, SB_WIDTH), BF16), pltpu.VMEM((tk, SB_WIDTH), BF16),
                        pltpu.VMEM((LANES, tk), F32), pltpu.VMEM((LANES, tk), F32),
                        pltpu.SemaphoreType.DMA((2,))],
        compiler_params=pltpu.CompilerParams(
            dimension_semantics=("arbitrary",), vmem_limit_bytes=VMEM_LIMIT),
        name="attn_sample",
    )(q_bf, k_bf, v_bf, ckt, cvt, ckt, cvt, tri, bias)


def _post_kernel(x_ref, att_ref, sga_ref, gyc_ref, p_ref,
                 wa_ref, wo_ref, gf_ref, wu_ref, wd_ref, gp_ref, wg_ref, wp_ref, gl_ref,
                 y_ref):
    tm = x_ref.shape[0]
    groups = [slice(r0, r0 + tm // POST_GROUPS) for r0 in range(0, tm, tm // POST_GROUPS)]
    each = lambda fn, *cols: [fn(*vals) for vals in zip(*cols)]
    ya = each(lambda rs: _dot(att_ref[rs], wa_ref[...]), groups)
    merged = each(lambda rs, ya: sga_ref[rs].astype(F32) * ya + gyc_ref[rs].astype(F32), groups, ya)
    x1 = each(lambda rs, m: x_ref[rs] + _dot(m.astype(BF16), wo_ref[...]), groups, merged)
    h2 = each(lambda x: _rms(x, gf_ref[...]).astype(BF16), x1)
    f = each(jnp.zeros_like, x1)
    for c in range(0, D_FF, FF_CHUNK):
        up = each(lambda h: jnp.maximum(_dot(h, wu_ref[:, c:c + FF_CHUNK]), 0.0), h2)
        f = each(lambda f, u: f + _dot((u * u).astype(BF16), wd_ref[c:c + FF_CHUNK, :]), f, up)
    x2 = each(lambda a, b: a + b, x1, f)
    h3 = each(lambda x: _rms(x, gp_ref[...]).astype(BF16), x2)
    gate = each(lambda h: jax.nn.sigmoid(_dot(h, wg_ref[...])), h3)
    x3 = each(lambda rs, x, g: x + g * _dot(p_ref[rs].astype(BF16), wp_ref[...]), groups, x2, gate)
    for rs, x in zip(groups, x3):
        y_ref[rs] = _rms(x, gl_ref[...])


def _post_call(x, att, sga, gyc, p, wa, wo, gf, wu, wd, gp, wg, wp, gl, name):
    n = x.shape[0]
    tm = min(POST_ROWS, n)
    row = lambda w: pl.BlockSpec((tm, w), lambda t: (t, 0))
    vec = lambda g: g.reshape(1, D_MODEL)
    return pl.pallas_call(
        _post_kernel,
        grid=(n // tm,),
        in_specs=[row(D_MODEL), row(SB_WIDTH), row(D_MODEL), row(D_MODEL), row(PLE_DIM),
                  _resident(wa.shape), _resident(wo.shape),
                  _resident((1, D_MODEL)), _resident(wu.shape), _resident(wd.shape),
                  _resident((1, D_MODEL)), _resident(wg.shape), _resident(wp.shape),
                  _resident((1, D_MODEL))],
        out_specs=row(D_MODEL),
        out_shape=jax.ShapeDtypeStruct((n, D_MODEL), F32),
        compiler_params=pltpu.CompilerParams(
            dimension_semantics=("arbitrary",), vmem_limit_bytes=VMEM_LIMIT),
        name=name,
    )(x, att, sga, gyc, p, wa, wo, vec(gf), wu, wd, vec(gp), wg, wp, vec(gl))


def _causal_bias(rows, cols, period, first_query=0):
    r = lax.broadcasted_iota(jnp.int32, (rows, cols), 0) % period + first_query
    c = lax.broadcasted_iota(jnp.int32, (rows, cols), 1)
    return jnp.where(c < r, 0.0, -jnp.inf).astype(F32)


def _prompt_bias():
    rows, cols = HEADS_PER_TILE * KEY_TILE, 2 * KEY_TILE
    c = lax.broadcasted_iota(jnp.int32, (rows, cols), 1)
    left_off = jnp.where(c < KEY_TILE, -jnp.inf, 0.0).astype(F32)
    return jnp.stack([_causal_bias(rows, cols, KEY_TILE, KEY_TILE), left_off])


def kernel(x_prompt, x_sample, p_prompt, p_sample, cache_k, cache_v, cache_conv,
           g_mix, w_in, conv_w, w_attn_out, w_conv_out, w_o,
           g_ffn, w_up, w_down, g_ple, w_ple_gate, w_ple, g_final):
    depth = w_in.shape[0]
    assert depth == 1
    b, t, _ = x_prompt.shape
    db, dt, _ = x_sample.shape
    past = cache_k.shape[2]
    bf = lambda w: w[0].astype(BF16)
    w_in_b, wa, wc, wo = bf(w_in), bf(w_attn_out), bf(w_conv_out), bf(w_o)
    wu, wd, wg, wp = bf(w_up), bf(w_down), bf(w_ple_gate), bf(w_ple)
    r = lax.broadcasted_iota(jnp.int32, (ATT_BLOCK, ATT_BLOCK), 0)
    c = lax.broadcasted_iota(jnp.int32, (ATT_BLOCK, ATT_BLOCK), 1)
    tri = (r >= c).astype(BF16)
    hd = (N_HEADS, HEAD_DIM)

    xp = x_prompt.reshape(b * t, D_MODEL)
    kpt, vpt, q_bf, kt_bf, v_bf, sga, gyc, conv_p = _proj_call(
        xp, g_mix[0], w_in_b, conv_w[0], wc, t)
    att = _attn_prompt_call(q_bf, kt_bf, v_bf, tri, _prompt_bias(), b, t)
    yp = _post_call(xp, att, sga, gyc, p_prompt[0].reshape(b * t, PLE_DIM),
                    wa, wo, g_ffn[0], wu, wd, g_ple[0], wg, wp, g_final, "post_prompt")
    kp = kpt.reshape(b, *hd, t).transpose(0, 3, 1, 2)
    vp = vpt.reshape(b, *hd, t).transpose(0, 3, 1, 2)

    xs = x_sample.reshape(db * dt, D_MODEL)
    buf = cache_conv[0]
    zeros = jnp.zeros((db, dt, CONV_WIDTH), F32)
    e1 = zeros.at[:, 0].set(buf[:, 1]).reshape(db * dt, CONV_WIDTH)
    e2 = zeros.at[:, 0].set(buf[:, 0]).at[:, 1].set(buf[:, 1]).reshape(db * dt, CONV_WIDTH)
    ks, vs, q_bf, k_bf, v_bf, sga, gyc, u_s = _proj_call(
        xs, g_mix[0], w_in_b, conv_w[0], wc, dt, hist=(e1, e2))
    ckt = cache_k[0].transpose(0, 2, 3, 1).reshape(db, SB_WIDTH, past)
    cvt = cache_v[0].transpose(0, 2, 3, 1).reshape(db, SB_WIDTH, past)
    att = _attn_sample_call(q_bf, k_bf, v_bf, ckt, cvt, tri,
                            _causal_bias(HEADS_PER_TILE * dt, ATT_BLOCK, dt), db, dt)
    ys = _post_call(xs, att, sga, gyc, p_sample[0].reshape(db * dt, PLE_DIM),
                    wa, wo, g_ffn[0], wu, wd, g_ple[0], wg, wp, g_final, "post_sample")
    conv_s = u_s.reshape(db, dt, CONV_WIDTH)[:, dt - (CONV_K - 1):]

    return (yp.reshape(b, t, D_MODEL), ys.reshape(db, dt, D_MODEL),
            kp[None], vp[None], conv_p[None],
            ks.reshape(1, db, dt, *hd), vs.reshape(1, db, dt, *hd), conv_s[None])
```

```python
import functools
import math

import jax
import jax.numpy as jnp
from jax import lax
from jax.experimental import pallas as pl
from jax.experimental.pallas import tpu as pltpu

F32 = jnp.float32
BF16 = jnp.bfloat16

D_MODEL = 1024
N_HEADS = 8
HEAD_DIM = 64
SB_WIDTH = N_HEADS * HEAD_DIM
CONV_WIDTH = 512
CONV_K = 3
PLE_DIM = 256
D_FF = 4 * D_MODEL
EPS = 1e-6
N_PROJ = 3 * SB_WIDTH + 3 * CONV_WIDTH + 2 * D_MODEL

LANES = 128
HEADS_PER_TILE = LANES // HEAD_DIM
N_PAIRS = N_HEADS // HEADS_PER_TILE
PROJ_ROWS = 1024
POST_ROWS = 512
ATT_BLOCK = 256
ATT_ROWS = 512
FF_CHUNK = 1024
POST_GROUPS = 2
CONV_PAD = 8
LOG2E = 1.4426950408889634
LOG2_ZERO = -126.0
KEY_TILE = LANES
VMEM_LIMIT = 52 * 1024 * 1024


def _resident(shape):
    nd = len(shape)
    return pl.BlockSpec(shape, lambda *_: (0,) * nd, pipeline_mode=pl.Buffered(1))


def _rms(x, g):
    return (x * lax.rsqrt(jnp.mean(x * x, axis=-1, keepdims=True) + EPS)) * g


def _dot(a, b):
    return jnp.dot(a, b, preferred_element_type=F32)


def _dot_nt(a, b):
    return lax.dot_general(a, b, (((1,), (1,)), ((), ())), preferred_element_type=F32)


def _proj_kernel(*refs, tiles_per_seq, seq_in_tile):
    prompt = seq_in_tile is None
    if prompt:
        (x_ref, g_ref, w_ref, cw_ref, wc_ref,
         k_out, v_out, q_bf, k_bf, v_bf, sga, gyc, cstate, s_ref) = refs
    else:
        (x_ref, g_ref, w_ref, cw_ref, wc_ref, e1_ref, e2_ref,
         k_out, v_out, q_bf, k_bf, v_bf, sga, gyc, cstate, s_ref) = refs
    tm = x_ref.shape[0]
    hb = _rms(x_ref[...], g_ref[...]).astype(BF16)
    o_q, o_k, o_v = 0, SB_WIDTH, 2 * SB_WIDTH
    o_cb = 3 * SB_WIDTH
    o_cc, o_cx = o_cb + CONV_WIDTH, o_cb + 2 * CONV_WIDTH
    o_ga = o_cb + 3 * CONV_WIDTH
    o_gc = o_ga + D_MODEL

    def proj(lo, width):
        return _dot(hb, w_ref[:, lo:lo + width])

    cb = proj(o_cb, CONV_WIDTH)
    u = proj(o_cc, CONV_WIDTH) * proj(o_cx, CONV_WIDTH)
    if prompt:
        @pl.when(pl.program_id(0) % tiles_per_seq == 0)
        def _():
            s_ref[0:CONV_PAD, :] = jnp.zeros((CONV_PAD, CONV_WIDTH), F32)
    else:
        s_ref[0:CONV_PAD, :] = jnp.zeros((CONV_PAD, CONV_WIDTH), F32)
    s_ref[CONV_PAD:CONV_PAD + tm, :] = u
    prev1 = s_ref[CONV_PAD - 1:CONV_PAD - 1 + tm, :]
    prev2 = s_ref[CONV_PAD - 2:CONV_PAD - 2 + tm, :]
    if prompt:
        s_ref[CONV_PAD - 2:CONV_PAD, :] = u[tm - 2:tm, :]
        cstate[0] = u[tm - 2:tm, :]
    else:
        r = lax.broadcasted_iota(jnp.int32, (tm, 1), 0) % seq_in_tile
        prev1 = jnp.where(r < 1, e1_ref[...], prev1)
        prev2 = jnp.where(r < 2, e2_ref[...], prev2)
        cstate[...] = u
    conv = cw_ref[0:1, :] * prev2 + cw_ref[1:2, :] * prev1 + cw_ref[2:3, :] * u
    cpre = (cb * conv).astype(BF16)

    sga[...] = jax.nn.sigmoid(proj(o_ga, D_MODEL)).astype(BF16)
    gyc[...] = (jax.nn.sigmoid(proj(o_gc, D_MODEL)) * _dot(cpre, wc_ref[...])).astype(BF16)

    k = proj(o_k, SB_WIDTH)
    v = proj(o_v, SB_WIDTH)
    if prompt:
        kt = k.T
        k_out[0] = kt
        for j in range(tm // KEY_TILE):
            k_bf[0, j] = kt[:, j * KEY_TILE:(j + 1) * KEY_TILE].astype(BF16)
        v_out[0] = v.T
    else:
        k_out[...] = k
        k_bf[...] = k.astype(BF16)
        v_out[...] = v
    v_bf[...] = v.astype(BF16)
    qb = (proj(o_q, SB_WIDTH) * (LOG2E / math.sqrt(HEAD_DIM))).astype(BF16)
    if prompt:
        first = lax.broadcasted_iota(jnp.int32, (1, SB_WIDTH), 1) % LANES < HEAD_DIM
        zero = jnp.zeros((), BF16)
        for j in range(tm // KEY_TILE):
            rows = qb[j * KEY_TILE:(j + 1) * KEY_TILE]
            base = j * HEADS_PER_TILE * KEY_TILE
            q_bf[base:base + KEY_TILE, :] = jnp.where(first, rows, zero)
            q_bf[base + KEY_TILE:base + 2 * KEY_TILE, :] = jnp.where(first, zero, rows)
    else:
        q_bf[...] = qb


def _proj_call(x, g, w_in, conv_w, w_conv_out, seq_len, hist=None):
    n = x.shape[0]
    tm = min(PROJ_ROWS, n)
    grid = (n // tm,)
    sample = hist is not None
    sds = jax.ShapeDtypeStruct
    row = lambda w: pl.BlockSpec((tm, w), lambda t: (t, 0))
    in_specs = [row(D_MODEL), _resident((1, D_MODEL)), _resident((D_MODEL, N_PROJ)),
                _resident((CONV_K, CONV_WIDTH)), _resident((CONV_WIDTH, D_MODEL))]
    args = [x, g.reshape(1, D_MODEL), w_in, conv_w, w_conv_out]
    if sample:
        in_specs += [row(CONV_WIDTH), row(CONV_WIDTH)]
        args += list(hist)
        tiles_per_seq, seq_in_tile = None, seq_len
        kv_shape = sds((n, SB_WIDTH), F32)
        kv_spec = row(SB_WIDTH)
        kbf_shape = sds((n, SB_WIDTH), BF16)
        kbf_spec = row(SB_WIDTH)
        cstate_shape = sds((n, CONV_WIDTH), F32)
        cstate_spec = row(CONV_WIDTH)
    else:
        tps = seq_len // tm
        nb = tm // KEY_TILE
        tiles_per_seq, seq_in_tile = tps, None
        batch = n // seq_len
        kv_shape = sds((batch, SB_WIDTH, seq_len), F32)
        kv_spec = pl.BlockSpec((1, SB_WIDTH, tm), lambda t: (t // tps, 0, t % tps))
        kbf_shape = sds((batch, seq_len // KEY_TILE, SB_WIDTH, KEY_TILE), BF16)
        kbf_spec = pl.BlockSpec((1, nb, SB_WIDTH, KEY_TILE), lambda t: (t // tps, t % tps, 0, 0))
        cstate_shape = sds((batch, CONV_K - 1, CONV_WIDTH), F32)
        cstate_spec = pl.BlockSpec((1, CONV_K - 1, CONV_WIDTH), lambda t: (t // tps, 0, 0))
    q_rows = 1 if sample else HEADS_PER_TILE
    out_shape = (kv_shape, kv_shape,
                 sds((q_rows * n, SB_WIDTH), BF16), kbf_shape, sds((n, SB_WIDTH), BF16),
                 sds((n, D_MODEL), BF16), sds((n, D_MODEL), BF16), cstate_shape)
    out_specs = (kv_spec, kv_spec, pl.BlockSpec((q_rows * tm, SB_WIDTH), lambda t: (t, 0)),
                 kbf_spec, row(SB_WIDTH), row(D_MODEL), row(D_MODEL), cstate_spec)
    return pl.pallas_call(
        functools.partial(_proj_kernel, tiles_per_seq=tiles_per_seq, seq_in_tile=seq_in_tile),
        grid=grid, in_specs=in_specs, out_specs=out_specs, out_shape=out_shape,
        scratch_shapes=[pltpu.VMEM((tm + CONV_PAD, CONV_WIDTH), F32)],
        compiler_params=pltpu.CompilerParams(
            dimension_semantics=("arbitrary",), vmem_limit_bytes=VMEM_LIMIT),
        name="proj_sample" if sample else "proj_prompt",
    )(*args)


def _softplus2(z):
    return jnp.maximum(z, 0.0) + jnp.log(1.0 + jnp.exp2(-jnp.abs(z))) * LOG2E


def _finish_block(z, vw, tri, spent, v_is_t=False):
    tail = _dot(_softplus2(z).astype(BF16), tri)
    a = jnp.exp2(z - tail - spent).astype(BF16)
    pv = _dot_nt(a, vw) if v_is_t else _dot(a, vw)
    return pv, spent + tail[:, 0:1]


def _stack_heads(qp):
    lane = lax.broadcasted_iota(jnp.int32, (1, LANES), 1)
    zero = jnp.zeros((), BF16)
    return jnp.concatenate([jnp.where(lane < HEAD_DIM, qp, zero),
                            jnp.where(lane >= HEAD_DIM, qp, zero)], axis=0)


def _unstack_heads(acc):
    lane = lax.broadcasted_iota(jnp.int32, (1, LANES), 1)
    half = acc.shape[0] // 2
    return jnp.where(lane < HEAD_DIM, acc[:half], acc[half:])


def _pair(p):
    return slice(p * LANES, (p + 1) * LANES)


def _attn_prompt_kernel(q_ref, kt_ref, v_ref, tri_ref, bias_ref, o_ref, spent_ref, acc_ref):
    i = pl.program_id(1)
    rows = HEADS_PER_TILE * KEY_TILE
    n_sub = q_ref.shape[0] // rows
    inst = [(p, s) for p in range(N_PAIRS) for s in range(n_sub)]

    def q2_of(p, s):
        return q_ref[s * rows:(s + 1) * rows, _pair(p)]

    def logits(p, s, jr, right_bias, jr_may_be_zero):
        jl = jnp.maximum(jr - 1, 0)
        kw = jnp.concatenate([kt_ref[0, jl, _pair(p), :], kt_ref[0, jr, _pair(p), :]], axis=1)
        z = _dot(q2_of(p, s), kw)
        zl, zr = z[:, :KEY_TILE], z[:, KEY_TILE:]
        if jr_may_be_zero:
            zl = zl + jnp.where(jr >= 1, 0.0, -jnp.inf)
        if right_bias is not None:
            zr = zr + right_bias
        rows_of = lambda j: pl.ds(pl.multiple_of(j * KEY_TILE, KEY_TILE), KEY_TILE)
        vw = jnp.concatenate([v_ref[rows_of(jl), _pair(p)], v_ref[rows_of(jr), _pair(p)]], axis=0)
        return jnp.concatenate([zl, zr], axis=1), vw

    zs, vws = [], []
    for p, s in inst:
        z, vw = logits(p, s, n_sub * i + s, bias_ref[...], s == 0)
        zs.append(z)
        vws.append(vw)
    sps = jnp.concatenate([_softplus2(z).astype(BF16) for z in zs], axis=0)
    tails = _dot(sps, tri_ref[...])
    least = []
    for n, (p, s) in enumerate(inst):
        tail = tails[n * rows:(n + 1) * rows]
        a = jnp.exp2(zs[n] - tail).astype(BF16)
        acc_ref[p, s] = _dot(a, vws[n])
        spent_ref[p, s] = tail[:, 0:1]
        least.append(jnp.min(tail[:, 0:1]))

    for n, (p, s) in enumerate(inst):
        def body(state, p=p, s=s):
            jr, _ = state
            z, vw = logits(p, s, jr, None, True)
            pv, spent = _finish_block(z, vw, tri_ref[...], spent_ref[p, s])
            acc_ref[p, s] += pv
            spent_ref[p, s] = spent
            return jr - 2, jnp.min(spent)

        lax.while_loop(lambda st: (st[0] >= 0) & (st[1] < -LOG2_ZERO), body,
                       (n_sub * i + s - 2, least[n]))

    for p, s in inst:
        o_ref[s * KEY_TILE:(s + 1) * KEY_TILE, _pair(p)] = _unstack_heads(acc_ref[p, s]).astype(BF16)


def _attn_prompt_call(q_bf, kt_bf, v_bf, tri, bias, batch, seq):
    tq = ATT_ROWS
    nq = seq // tq
    n_sub = tq // KEY_TILE
    rows = HEADS_PER_TILE * KEY_TILE
    return pl.pallas_call(
        _attn_prompt_kernel,
        grid=(batch, nq),
        in_specs=[pl.BlockSpec((HEADS_PER_TILE * tq, SB_WIDTH), lambda b, i: (b * nq + i, 0)),
                  pl.BlockSpec((1, seq // KEY_TILE, SB_WIDTH, KEY_TILE), lambda b, i: (b, 0, 0, 0)),
                  pl.BlockSpec((seq, SB_WIDTH), lambda b, i: (b, 0)),
                  _resident(tri.shape), _resident(bias.shape)],
        out_specs=pl.BlockSpec((tq, SB_WIDTH), lambda b, i: (b * nq + i, 0)),
        out_shape=jax.ShapeDtypeStruct((batch * seq, SB_WIDTH), BF16),
        scratch_shapes=[pltpu.VMEM((N_PAIRS, n_sub, rows, 1), F32),
                        pltpu.VMEM((N_PAIRS, n_sub, rows, LANES), F32)],
        compiler_params=pltpu.CompilerParams(
            dimension_semantics=("arbitrary", "arbitrary"), vmem_limit_bytes=VMEM_LIMIT),
        name="attn_prompt",
    )(q_bf, kt_bf, v_bf, tri, bias)


def _attn_sample_kernel(q_ref, kn_ref, vn_ref, ck0_ref, cv0_ref, ck_any, cv_any, tri_ref, bias_ref,
                        o_ref, spent_ref, acc_ref, kpad_ref, vpad_ref, kbuf_ref, vbuf_ref, sem):
    b = pl.program_id(0)
    tq = q_ref.shape[0]
    tk = tri_ref.shape[0]
    n_win = ck_any.shape[2] // tk
    rows = HEADS_PER_TILE * tq
    kpad_ref[...] = jnp.zeros_like(kpad_ref)
    vpad_ref[...] = jnp.zeros_like(vpad_ref)
    kpad_ref[0:tq, :] = kn_ref[...]
    vpad_ref[0:tq, :] = vn_ref[...]
    zs = []
    for p in range(N_PAIRS):
        q2 = _stack_heads(q_ref[:, _pair(p)])
        zs.append(_dot_nt(q2, kpad_ref[:, _pair(p)]) + bias_ref[...])
        zs.append(_dot(q2, ck0_ref[0, _pair(p), :].astype(BF16)))
    tails = _dot(jnp.concatenate([_softplus2(z).astype(BF16) for z in zs], axis=0), tri_ref[...])
    least = []
    for p in range(N_PAIRS):
        t_new = tails[(2 * p) * rows:(2 * p + 1) * rows]
        t_old = tails[(2 * p + 1) * rows:(2 * p + 2) * rows]
        a_new = jnp.exp2(zs[2 * p] - t_new).astype(BF16)
        a_old = jnp.exp2(zs[2 * p + 1] - t_old - t_new[:, 0:1]).astype(BF16)
        acc_ref[p] = (_dot(a_new, vpad_ref[:, _pair(p)])
                      + _dot_nt(a_old, cv0_ref[0, _pair(p), :].astype(BF16)))
        spent = t_new[:, 0:1] + t_old[:, 0:1]
        spent_ref[p] = spent
        least.append(jnp.min(spent))
    for p in range(N_PAIRS):
        def window(src, dst, j, slot, p=p):
            k0 = pl.multiple_of(j * tk, tk)
            return pltpu.make_async_copy(src.at[b, _pair(p), pl.ds(k0, tk)], dst, sem.at[slot])

        def body(state, p=p, window=window):
            j, _ = state
            window(ck_any, kbuf_ref, j, 0).start()
            window(cv_any, vbuf_ref, j, 1).start()
            window(ck_any, kbuf_ref, j, 0).wait()
            window(cv_any, vbuf_ref, j, 1).wait()
            z = _dot(_stack_heads(q_ref[:, _pair(p)]), kbuf_ref[...].astype(BF16))
            pv, spent = _finish_block(z, vbuf_ref[...].astype(BF16), tri_ref[...], spent_ref[p], True)
            acc_ref[p] += pv
            spent_ref[p] = spent
            return j - 1, jnp.min(spent)

        lax.while_loop(lambda s: (s[0] >= 0) & (s[1] < -LOG2_ZERO), body, (n_win - 2, least[p]))
    for p in range(N_PAIRS):
        o_ref[:, _pair(p)] = _unstack_heads(acc_ref[p]).astype(BF16)


def _attn_sample_call(q_bf, k_bf, v_bf, ckt, cvt, tri, bias, batch, seq):
    past = ckt.shape[2]
    tk = tri.shape[0]
    n_win = past // tk
    row = pl.BlockSpec((seq, SB_WIDTH), lambda b: (b, 0))
    last = pl.BlockSpec((1, SB_WIDTH, tk), lambda b: (b, 0, n_win - 1))
    hbm = pl.BlockSpec(memory_space=pl.ANY)
    return pl.pallas_call(
        _attn_sample_kernel,
        grid=(batch,),
        in_specs=[row, row, row, last, last, hbm, hbm, _resident(tri.shape), _resident(bias.shape)],
        out_specs=row,
        out_shape=jax.ShapeDtypeStruct((batch * seq, SB_WIDTH), BF16),
        scratch_shapes=[pltpu.VMEM((N_PAIRS, 2 * seq, 1), F32),
                        pltpu.VMEM((N_PAIRS, 2 * seq, LANES), F32),
                        pltpu.VMEM((tk, SB_WIDTH), BF16), pltpu.VMEM((tk, SB_WIDTH), BF16),
                        pltpu.VMEM((LANES, tk), F32), pltpu.VMEM((LANES, tk), F32),
                        pltpu.SemaphoreType.DMA((2,))],
        compiler_params=pltpu.CompilerParams(
            dimension_semantics=("arbitrary",), vmem_limit_bytes=VMEM_LIMIT),
        name="attn_sample",
    )(q_bf, k_bf, v_bf, ckt, cvt, ckt, cvt, tri, bias)


def _post_kernel(x_ref, att_ref, sga_ref, gyc_ref, p_ref,
                 wa_ref, wo_ref, gf_ref, wu_ref, wd_ref, gp_ref, wg_ref, wp_ref, gl_ref,
                 y_ref):
    tm = x_ref.shape[0]
    groups = [slice(r0, r0 + tm // POST_GROUPS) for r0 in range(0, tm, tm // POST_GROUPS)]
    each = lambda fn, *cols: [fn(*vals) for vals in zip(*cols)]
    ya = each(lambda rs: _dot(att_ref[rs], wa_ref[...]), groups)
    merged = each(lambda rs, ya: sga_ref[rs].astype(F32) * ya + gyc_ref[rs].astype(F32), groups, ya)
    x1 = each(lambda rs, m: x_ref[rs] + _dot(m.astype(BF16), wo_ref[...]), groups, merged)
    h2 = each(lambda x: _rms(x, gf_ref[...]).astype(BF16), x1)
    f = each(jnp.zeros_like, x1)
    for c in range(0, D_FF, FF_CHUNK):
        up = each(lambda h: jnp.maximum(_dot(h, wu_ref[:, c:c + FF_CHUNK]), 0.0), h2)
        f = each(lambda f, u: f + _dot((u * u).astype(BF16), wd_ref[c:c + FF_CHUNK, :]), f, up)
    x2 = each(lambda a, b: a + b, x1, f)
    h3 = each(lambda x: _rms(x, gp_ref[...]).astype(BF16), x2)
    gate = each(lambda h: jax.nn.sigmoid(_dot(h, wg_ref[...])), h3)
    x3 = each(lambda rs, x, g: x + g * _dot(p_ref[rs].astype(BF16), wp_ref[...]), groups, x2, gate)
    for rs, x in zip(groups, x3):
        y_ref[rs] = _rms(x, gl_ref[...])


def _post_call(x, att, sga, gyc, p, wa, wo, gf, wu, wd, gp, wg, wp, gl, name):
    n = x.shape[0]
    tm = min(POST_ROWS, n)
    row = lambda w: pl.BlockSpec((tm, w), lambda t: (t, 0))
    vec = lambda g: g.reshape(1, D_MODEL)
    return pl.pallas_call(
        _post_kernel,
        grid=(n // tm,),
        in_specs=[row(D_MODEL), row(SB_WIDTH), row(D_MODEL), row(D_MODEL), row(PLE_DIM),
                  _resident(wa.shape), _resident(wo.shape),
                  _resident((1, D_MODEL)), _resident(wu.shape), _resident(wd.shape),
                  _resident((1, D_MODEL)), _resident(wg.shape), _resident(wp.shape),
                  _resident((1, D_MODEL))],
        out_specs=row(D_MODEL),
        out_shape=jax.ShapeDtypeStruct((n, D_MODEL), F32),
        compiler_params=pltpu.CompilerParams(
            dimension_semantics=("arbitrary",), vmem_limit_bytes=VMEM_LIMIT),
        name=name,
    )(x, att, sga, gyc, p, wa, wo, vec(gf), wu, wd, vec(gp), wg, wp, vec(gl))


def _causal_bias(rows, cols, period):
    r = lax.broadcasted_iota(jnp.int32, (rows, cols), 0) % period
    c = lax.broadcasted_iota(jnp.int32, (rows, cols), 1)
    return jnp.where(c < r, 0.0, -jnp.inf).astype(F32)


def kernel(x_prompt, x_sample, p_prompt, p_sample, cache_k, cache_v, cache_conv,
           g_mix, w_in, conv_w, w_attn_out, w_conv_out, w_o,
           g_ffn, w_up, w_down, g_ple, w_ple_gate, w_ple, g_final):
    depth = w_in.shape[0]
    assert depth == 1
    b, t, _ = x_prompt.shape
    db, dt, _ = x_sample.shape
    past = cache_k.shape[2]
    bf = lambda w: w[0].astype(BF16)
    w_in_b, wa, wc, wo = bf(w_in), bf(w_attn_out), bf(w_conv_out), bf(w_o)
    wu, wd, wg, wp = bf(w_up), bf(w_down), bf(w_ple_gate), bf(w_ple)
    r = lax.broadcasted_iota(jnp.int32, (ATT_BLOCK, ATT_BLOCK), 0)
    c = lax.broadcasted_iota(jnp.int32, (ATT_BLOCK, ATT_BLOCK), 1)
    tri = (r >= c).astype(BF16)
    hd = (N_HEADS, HEAD_DIM)

    xp = x_prompt.reshape(b * t, D_MODEL)
    kpt, vpt, q_bf, kt_bf, v_bf, sga, gyc, conv_p = _proj_call(
        xp, g_mix[0], w_in_b, conv_w[0], wc, t)
    att = _attn_prompt_call(q_bf, kt_bf, v_bf, tri,
                            _causal_bias(HEADS_PER_TILE * KEY_TILE, KEY_TILE, KEY_TILE), b, t)
    yp = _post_call(xp, att, sga, gyc, p_prompt[0].reshape(b * t, PLE_DIM),
                    wa, wo, g_ffn[0], wu, wd, g_ple[0], wg, wp, g_final, "post_prompt")
    kp = kpt.reshape(b, *hd, t).transpose(0, 3, 1, 2)
    vp = vpt.reshape(b, *hd, t).transpose(0, 3, 1, 2)

    xs = x_sample.reshape(db * dt, D_MODEL)
    buf = cache_conv[0]
    zeros = jnp.zeros((db, dt, CONV_WIDTH), F32)
    e1 = zeros.at[:, 0].set(buf[:, 1]).reshape(db * dt, CONV_WIDTH)
    e2 = zeros.at[:, 0].set(buf[:, 0]).at[:, 1].set(buf[:, 1]).reshape(db * dt, CONV_WIDTH)
    ks, vs, q_bf, k_bf, v_bf, sga, gyc, u_s = _proj_call(
        xs, g_mix[0], w_in_b, conv_w[0], wc, dt, hist=(e1, e2))
    ckt = cache_k[0].transpose(0, 2, 3, 1).reshape(db, SB_WIDTH, past)
    cvt = cache_v[0].transpose(0, 2, 3, 1).reshape(db, SB_WIDTH, past)
    att = _attn_sample_call(q_bf, k_bf, v_bf, ckt, cvt, tri,
                            _causal_bias(HEADS_PER_TILE * dt, ATT_BLOCK, dt), db, dt)
    ys = _post_call(xs, att, sga, gyc, p_sample[0].reshape(db * dt, PLE_DIM),
                    wa, wo, g_ffn[0], wu, wd, g_ple[0], wg, wp, g_final, "post_sample")
    conv_s = u_s.reshape(db, dt, CONV_WIDTH)[:, dt - (CONV_K - 1):]

    return (yp.reshape(b, t, D_MODEL), ys.reshape(db, dt, D_MODEL),
            kp[None], vp[None], conv_p[None],
            ks.reshape(1, db, dt, *hd), vs.reshape(1, db, dt, *hd), conv_s[None])
```

```python
import functools
import math

import jax
import jax.numpy as jnp
from jax import lax
from jax.experimental import pallas as pl
from jax.experimental.pallas import tpu as pltpu

F32 = jnp.float32
BF16 = jnp.bfloat16

D_MODEL = 1024
N_HEADS = 8
HEAD_DIM = 64
SB_WIDTH = N_HEADS * HEAD_DIM
CONV_WIDTH = 512
CONV_K = 3
PLE_DIM = 256
D_FF = 4 * D_MODEL
EPS = 1e-6
N_PROJ = 3 * SB_WIDTH + 3 * CONV_WIDTH + 2 * D_MODEL

LANES = 128
HEADS_PER_TILE = LANES // HEAD_DIM
N_PAIRS = N_HEADS // HEADS_PER_TILE
PROJ_ROWS = 1024
POST_ROWS = 512
ATT_BLOCK = 256
ATT_ROWS = 512
SAMPLE_GROUP = 4
FF_CHUNK = 1024
POST_GROUPS = 2
CONV_PAD = 8
LOG2E = 1.4426950408889634
LOG2_ZERO = -126.0
KEY_TILE = LANES
VMEM_LIMIT = 52 * 1024 * 1024


def _resident(shape):
    nd = len(shape)
    return pl.BlockSpec(shape, lambda *_: (0,) * nd, pipeline_mode=pl.Buffered(1))


def _rms(x, g):
    return (x * lax.rsqrt(jnp.mean(x * x, axis=-1, keepdims=True) + EPS)) * g


def _dot(a, b):
    return jnp.dot(a, b, preferred_element_type=F32)


def _dot_nt(a, b):
    return lax.dot_general(a, b, (((1,), (1,)), ((), ())), preferred_element_type=F32)


def _proj_kernel(*refs, tiles_per_seq, seq_in_tile):
    prompt = seq_in_tile is None
    if prompt:
        (x_ref, g_ref, w_ref, cw_ref, wc_ref,
         k_out, v_out, q_bf, k_bf, v_bf, sga, gyc, cstate, s_ref) = refs
    else:
        (x_ref, g_ref, w_ref, cw_ref, wc_ref, e1_ref, e2_ref,
         k_out, v_out, q_bf, k_bf, v_bf, sga, gyc, cstate, s_ref) = refs
    tm = x_ref.shape[0]
    hb = _rms(x_ref[...], g_ref[...]).astype(BF16)
    o_q, o_k, o_v = 0, SB_WIDTH, 2 * SB_WIDTH
    o_cb = 3 * SB_WIDTH
    o_cc, o_cx = o_cb + CONV_WIDTH, o_cb + 2 * CONV_WIDTH
    o_ga = o_cb + 3 * CONV_WIDTH
    o_gc = o_ga + D_MODEL

    def proj(lo, width):
        return _dot(hb, w_ref[:, lo:lo + width])

    cb = proj(o_cb, CONV_WIDTH)
    u = proj(o_cc, CONV_WIDTH) * proj(o_cx, CONV_WIDTH)
    if prompt:
        @pl.when(pl.program_id(0) % tiles_per_seq == 0)
        def _():
            s_ref[0:CONV_PAD, :] = jnp.zeros((CONV_PAD, CONV_WIDTH), F32)
    else:
        s_ref[0:CONV_PAD, :] = jnp.zeros((CONV_PAD, CONV_WIDTH), F32)
    s_ref[CONV_PAD:CONV_PAD + tm, :] = u
    prev1 = s_ref[CONV_PAD - 1:CONV_PAD - 1 + tm, :]
    prev2 = s_ref[CONV_PAD - 2:CONV_PAD - 2 + tm, :]
    if prompt:
        s_ref[CONV_PAD - 2:CONV_PAD, :] = u[tm - 2:tm, :]
        cstate[0] = u[tm - 2:tm, :]
    else:
        r = lax.broadcasted_iota(jnp.int32, (tm, 1), 0) % seq_in_tile
        prev1 = jnp.where(r < 1, e1_ref[...], prev1)
        prev2 = jnp.where(r < 2, e2_ref[...], prev2)
        cstate[...] = u
    conv = cw_ref[0:1, :] * prev2 + cw_ref[1:2, :] * prev1 + cw_ref[2:3, :] * u
    cpre = (cb * conv).astype(BF16)

    sga[...] = jax.nn.sigmoid(proj(o_ga, D_MODEL)).astype(BF16)
    gyc[...] = (jax.nn.sigmoid(proj(o_gc, D_MODEL)) * _dot(cpre, wc_ref[...])).astype(BF16)

    k = proj(o_k, SB_WIDTH)
    v = proj(o_v, SB_WIDTH)
    if prompt:
        kt = k.T
        k_out[0] = kt
        for j in range(tm // KEY_TILE):
            k_bf[0, j] = kt[:, j * KEY_TILE:(j + 1) * KEY_TILE].astype(BF16)
        v_out[0] = v.T
    else:
        k_out[...] = k
        k_bf[...] = k.astype(BF16)
        v_out[...] = v
    v_bf[...] = v.astype(BF16)
    qb = (proj(o_q, SB_WIDTH) * (LOG2E / math.sqrt(HEAD_DIM))).astype(BF16)
    if prompt:
        first = lax.broadcasted_iota(jnp.int32, (1, SB_WIDTH), 1) % LANES < HEAD_DIM
        zero = jnp.zeros((), BF16)
        for j in range(tm // KEY_TILE):
            rows = qb[j * KEY_TILE:(j + 1) * KEY_TILE]
            base = j * HEADS_PER_TILE * KEY_TILE
            q_bf[base:base + KEY_TILE, :] = jnp.where(first, rows, zero)
            q_bf[base + KEY_TILE:base + 2 * KEY_TILE, :] = jnp.where(first, zero, rows)
    else:
        q_bf[...] = qb


def _proj_call(x, g, w_in, conv_w, w_conv_out, seq_len, hist=None):
    n = x.shape[0]
    tm = min(PROJ_ROWS, n)
    grid = (n // tm,)
    sample = hist is not None
    sds = jax.ShapeDtypeStruct
    row = lambda w: pl.BlockSpec((tm, w), lambda t: (t, 0))
    in_specs = [row(D_MODEL), _resident((1, D_MODEL)), _resident((D_MODEL, N_PROJ)),
                _resident((CONV_K, CONV_WIDTH)), _resident((CONV_WIDTH, D_MODEL))]
    args = [x, g.reshape(1, D_MODEL), w_in, conv_w, w_conv_out]
    if sample:
        in_specs += [row(CONV_WIDTH), row(CONV_WIDTH)]
        args += list(hist)
        tiles_per_seq, seq_in_tile = None, seq_len
        kv_shape = sds((n, SB_WIDTH), F32)
        kv_spec = row(SB_WIDTH)
        kbf_shape = sds((n, SB_WIDTH), BF16)
        kbf_spec = row(SB_WIDTH)
        cstate_shape = sds((n, CONV_WIDTH), F32)
        cstate_spec = row(CONV_WIDTH)
    else:
        tps = seq_len // tm
        nb = tm // KEY_TILE
        tiles_per_seq, seq_in_tile = tps, None
        batch = n // seq_len
        kv_shape = sds((batch, SB_WIDTH, seq_len), F32)
        kv_spec = pl.BlockSpec((1, SB_WIDTH, tm), lambda t: (t // tps, 0, t % tps))
        kbf_shape = sds((batch, seq_len // KEY_TILE, SB_WIDTH, KEY_TILE), BF16)
        kbf_spec = pl.BlockSpec((1, nb, SB_WIDTH, KEY_TILE), lambda t: (t // tps, t % tps, 0, 0))
        cstate_shape = sds((batch, CONV_K - 1, CONV_WIDTH), F32)
        cstate_spec = pl.BlockSpec((1, CONV_K - 1, CONV_WIDTH), lambda t: (t // tps, 0, 0))
    q_rows = 1 if sample else HEADS_PER_TILE
    out_shape = (kv_shape, kv_shape,
                 sds((q_rows * n, SB_WIDTH), BF16), kbf_shape, sds((n, SB_WIDTH), BF16),
                 sds((n, D_MODEL), BF16), sds((n, D_MODEL), BF16), cstate_shape)
    out_specs = (kv_spec, kv_spec, pl.BlockSpec((q_rows * tm, SB_WIDTH), lambda t: (t, 0)),
                 kbf_spec, row(SB_WIDTH), row(D_MODEL), row(D_MODEL), cstate_spec)
    return pl.pallas_call(
        functools.partial(_proj_kernel, tiles_per_seq=tiles_per_seq, seq_in_tile=seq_in_tile),
        grid=grid, in_specs=in_specs, out_specs=out_specs, out_shape=out_shape,
        scratch_shapes=[pltpu.VMEM((tm + CONV_PAD, CONV_WIDTH), F32)],
        compiler_params=pltpu.CompilerParams(
            dimension_semantics=("arbitrary",), vmem_limit_bytes=VMEM_LIMIT),
        name="proj_sample" if sample else "proj_prompt",
    )(*args)


def _softplus2(z):
    return jnp.maximum(z, 0.0) + jnp.log(1.0 + jnp.exp2(-jnp.abs(z))) * LOG2E


def _finish_block(z, vw, tri, spent, v_is_t=False):
    tail = _dot(_softplus2(z).astype(BF16), tri)
    a = jnp.exp2(z - tail - spent).astype(BF16)
    pv = _dot_nt(a, vw) if v_is_t else _dot(a, vw)
    return pv, spent + tail[:, 0:1]


def _stack_heads(qp):
    lane = lax.broadcasted_iota(jnp.int32, (1, LANES), 1)
    zero = jnp.zeros((), BF16)
    return jnp.concatenate([jnp.where(lane < HEAD_DIM, qp, zero),
                            jnp.where(lane >= HEAD_DIM, qp, zero)], axis=0)


def _unstack_heads(acc):
    lane = lax.broadcasted_iota(jnp.int32, (1, LANES), 1)
    half = acc.shape[0] // 2
    return jnp.where(lane < HEAD_DIM, acc[:half], acc[half:])


def _pair(p):
    return slice(p * LANES, (p + 1) * LANES)


def _attn_prompt_kernel(q_ref, kt_ref, v_ref, tri_ref, bias_ref, o_ref, spent_ref, acc_ref):
    i = pl.program_id(1)
    rows = HEADS_PER_TILE * KEY_TILE
    n_sub = q_ref.shape[0] // rows
    inst = [(p, s) for p in range(N_PAIRS) for s in range(n_sub)]

    def q2_of(p, s):
        return q_ref[s * rows:(s + 1) * rows, _pair(p)]

    def logits(p, s, jr, right_bias, jr_may_be_zero):
        jl = jnp.maximum(jr - 1, 0)
        kw = jnp.concatenate([kt_ref[0, jl, _pair(p), :], kt_ref[0, jr, _pair(p), :]], axis=1)
        z = _dot(q2_of(p, s), kw)
        zl, zr = z[:, :KEY_TILE], z[:, KEY_TILE:]
        if jr_may_be_zero:
            zl = zl + jnp.where(jr >= 1, 0.0, -jnp.inf)
        if right_bias is not None:
            zr = zr + right_bias
        rows_of = lambda j: pl.ds(pl.multiple_of(j * KEY_TILE, KEY_TILE), KEY_TILE)
        vw = jnp.concatenate([v_ref[rows_of(jl), _pair(p)], v_ref[rows_of(jr), _pair(p)]], axis=0)
        return jnp.concatenate([zl, zr], axis=1), vw

    zs, vws = [], []
    for p, s in inst:
        z, vw = logits(p, s, n_sub * i + s, bias_ref[...], s == 0)
        zs.append(z)
        vws.append(vw)
    sps = jnp.concatenate([_softplus2(z).astype(BF16) for z in zs], axis=0)
    tails = _dot(sps, tri_ref[...])
    least = []
    for n, (p, s) in enumerate(inst):
        tail = tails[n * rows:(n + 1) * rows]
        a = jnp.exp2(zs[n] - tail).astype(BF16)
        acc_ref[p, s] = _dot(a, vws[n])
        spent_ref[p, s] = tail[:, 0:1]
        least.append(jnp.min(tail[:, 0:1]))

    for n, (p, s) in enumerate(inst):
        def body(state, p=p, s=s):
            jr, _ = state
            z, vw = logits(p, s, jr, None, True)
            pv, spent = _finish_block(z, vw, tri_ref[...], spent_ref[p, s])
            acc_ref[p, s] += pv
            spent_ref[p, s] = spent
            return jr - 2, jnp.min(spent)

        lax.while_loop(lambda st: (st[0] >= 0) & (st[1] < -LOG2_ZERO), body,
                       (n_sub * i + s - 2, least[n]))

    for p, s in inst:
        o_ref[s * KEY_TILE:(s + 1) * KEY_TILE, _pair(p)] = _unstack_heads(acc_ref[p, s]).astype(BF16)


def _attn_prompt_call(q_bf, kt_bf, v_bf, tri, bias, batch, seq):
    tq = ATT_ROWS
    nq = seq // tq
    n_sub = tq // KEY_TILE
    rows = HEADS_PER_TILE * KEY_TILE
    return pl.pallas_call(
        _attn_prompt_kernel,
        grid=(batch, nq),
        in_specs=[pl.BlockSpec((HEADS_PER_TILE * tq, SB_WIDTH), lambda b, i: (b * nq + i, 0)),
                  pl.BlockSpec((1, seq // KEY_TILE, SB_WIDTH, KEY_TILE), lambda b, i: (b, 0, 0, 0)),
                  pl.BlockSpec((seq, SB_WIDTH), lambda b, i: (b, 0)),
                  _resident(tri.shape), _resident(bias.shape)],
        out_specs=pl.BlockSpec((tq, SB_WIDTH), lambda b, i: (b * nq + i, 0)),
        out_shape=jax.ShapeDtypeStruct((batch * seq, SB_WIDTH), BF16),
        scratch_shapes=[pltpu.VMEM((N_PAIRS, n_sub, rows, 1), F32),
                        pltpu.VMEM((N_PAIRS, n_sub, rows, LANES), F32)],
        compiler_params=pltpu.CompilerParams(
            dimension_semantics=("arbitrary", "arbitrary"), vmem_limit_bytes=VMEM_LIMIT),
        name="attn_prompt",
    )(q_bf, kt_bf, v_bf, tri, bias)


def _attn_sample_kernel(q_ref, kn_ref, vn_ref, ck0_ref, cv0_ref, ck_any, cv_any, tri_ref, bias_ref,
                        o_ref, spent_ref, acc_ref, kpad_ref, vpad_ref, kbuf_ref, vbuf_ref, sem, *, seq):
    g = pl.program_id(0)
    n_el = q_ref.shape[0] // seq
    tk = tri_ref.shape[0]
    n_win = ck_any.shape[2] // tk
    rows = HEADS_PER_TILE * seq
    inst = [(e, p) for e in range(n_el) for p in range(N_PAIRS)]
    el = lambda e: slice(e * seq, (e + 1) * seq)
    kpad_ref[...] = jnp.zeros_like(kpad_ref)
    vpad_ref[...] = jnp.zeros_like(vpad_ref)
    for e in range(n_el):
        kpad_ref[e, 0:seq, :] = kn_ref[el(e), :]
        vpad_ref[e, 0:seq, :] = vn_ref[el(e), :]
    zs = []
    for e, p in inst:
        q2 = _stack_heads(q_ref[el(e), _pair(p)])
        zs.append(_dot_nt(q2, kpad_ref[e, :, _pair(p)]) + bias_ref[...])
        zs.append(_dot(q2, ck0_ref[e, _pair(p), :].astype(BF16)))
    tails = _dot(jnp.concatenate([_softplus2(z).astype(BF16) for z in zs], axis=0), tri_ref[...])
    least = []
    for n, (e, p) in enumerate(inst):
        t_new = tails[(2 * n) * rows:(2 * n + 1) * rows]
        t_old = tails[(2 * n + 1) * rows:(2 * n + 2) * rows]
        a_new = jnp.exp2(zs[2 * n] - t_new).astype(BF16)
        a_old = jnp.exp2(zs[2 * n + 1] - t_old - t_new[:, 0:1]).astype(BF16)
        acc_ref[e, p] = (_dot(a_new, vpad_ref[e, :, _pair(p)])
                         + _dot_nt(a_old, cv0_ref[e, _pair(p), :].astype(BF16)))
        spent = t_new[:, 0:1] + t_old[:, 0:1]
        spent_ref[e, p] = spent
        least.append(jnp.min(spent))
    for n, (e, p) in enumerate(inst):
        def window(src, dst, j, slot, e=e, p=p):
            k0 = pl.multiple_of(j * tk, tk)
            return pltpu.make_async_copy(src.at[g * n_el + e, _pair(p), pl.ds(k0, tk)], dst, sem.at[slot])

        def body(state, e=e, p=p, window=window):
            j, _ = state
            window(ck_any, kbuf_ref, j, 0).start()
            window(cv_any, vbuf_ref, j, 1).start()
            window(ck_any, kbuf_ref, j, 0).wait()
            window(cv_any, vbuf_ref, j, 1).wait()
            z = _dot(_stack_heads(q_ref[el(e), _pair(p)]), kbuf_ref[...].astype(BF16))
            pv, spent = _finish_block(z, vbuf_ref[...].astype(BF16), tri_ref[...], spent_ref[e, p], True)
            acc_ref[e, p] += pv
            spent_ref[e, p] = spent
            return j - 1, jnp.min(spent)

        lax.while_loop(lambda s: (s[0] >= 0) & (s[1] < -LOG2_ZERO), body, (n_win - 2, least[n]))
    for e, p in inst:
        o_ref[el(e), _pair(p)] = _unstack_heads(acc_ref[e, p]).astype(BF16)


def _attn_sample_call(q_bf, k_bf, v_bf, ckt, cvt, tri, bias, batch, seq):
    past = ckt.shape[2]
    tk = tri.shape[0]
    n_win = past // tk
    n_el = SAMPLE_GROUP
    rows = HEADS_PER_TILE * seq
    row = pl.BlockSpec((n_el * seq, SB_WIDTH), lambda g: (g, 0))
    last = pl.BlockSpec((n_el, SB_WIDTH, tk), lambda g: (g, 0, n_win - 1))
    hbm = pl.BlockSpec(memory_space=pl.ANY)
    return pl.pallas_call(
        functools.partial(_attn_sample_kernel, seq=seq),
        grid=(batch // n_el,),
        in_specs=[row, row, row, last, last, hbm, hbm, _resident(tri.shape), _resident(bias.shape)],
        out_specs=row,
        out_shape=jax.ShapeDtypeStruct((batch * seq, SB_WIDTH), BF16),
        scratch_shapes=[pltpu.VMEM((n_el, N_PAIRS, rows, 1), F32),
                        pltpu.VMEM((n_el, N_PAIRS, rows, LANES), F32),
                        pltpu.VMEM((n_el, tk, SB_WIDTH), BF16), pltpu.VMEM((n_el, tk, SB_WIDTH), BF16),
                        pltpu.VMEM((LANES, tk), F32), pltpu.VMEM((LANES, tk), F32),
                        pltpu.SemaphoreType.DMA((2,))],
        compiler_params=pltpu.CompilerParams(
            dimension_semantics=("arbitrary",), vmem_limit_bytes=VMEM_LIMIT),
        name="attn_sample",
    )(q_bf, k_bf, v_bf, ckt, cvt, ckt, cvt, tri, bias)


def _post_kernel(x_ref, att_ref, sga_ref, gyc_ref, p_ref,
                 wa_ref, wo_ref, gf_ref, wu_ref, wd_ref, gp_ref, wg_ref, wp_ref, gl_ref,
                 y_ref):
    tm = x_ref.shape[0]
    groups = [slice(r0, r0 + tm // POST_GROUPS) for r0 in range(0, tm, tm // POST_GROUPS)]
    each = lambda fn, *cols: [fn(*vals) for vals in zip(*cols)]
    ya = each(lambda rs: _dot(att_ref[rs], wa_ref[...]), groups)
    merged = each(lambda rs, ya: sga_ref[rs].astype(F32) * ya + gyc_ref[rs].astype(F32), groups, ya)
    x1 = each(lambda rs, m: x_ref[rs] + _dot(m.astype(BF16), wo_ref[...]), groups, merged)
    h2 = each(lambda x: _rms(x, gf_ref[...]).astype(BF16), x1)
    f = each(jnp.zeros_like, x1)
    for c in range(0, D_FF, FF_CHUNK):
        up = each(lambda h: jnp.maximum(_dot(h, wu_ref[:, c:c + FF_CHUNK]), 0.0), h2)
        f = each(lambda f, u: f + _dot((u * u).astype(BF16), wd_ref[c:c + FF_CHUNK, :]), f, up)
    x2 = each(lambda a, b: a + b, x1, f)
    h3 = each(lambda x: _rms(x, gp_ref[...]).astype(BF16), x2)
    gate = each(lambda h: jax.nn.sigmoid(_dot(h, wg_ref[...])), h3)
    x3 = each(lambda rs, x, g: x + g * _dot(p_ref[rs].astype(BF16), wp_ref[...]), groups, x2, gate)
    for rs, x in zip(groups, x3):
        y_ref[rs] = _rms(x, gl_ref[...])


def _post_call(x, att, sga, gyc, p, wa, wo, gf, wu, wd, gp, wg, wp, gl, name):
    n = x.shape[0]
    tm = min(POST_ROWS, n)
    row = lambda w: pl.BlockSpec((tm, w), lambda t: (t, 0))
    vec = lambda g: g.reshape(1, D_MODEL)
    return pl.pallas_call(
        _post_kernel,
        grid=(n // tm,),
        in_specs=[row(D_MODEL), row(SB_WIDTH), row(D_MODEL), row(D_MODEL), row(PLE_DIM),
                  _resident(wa.shape), _resident(wo.shape),
                  _resident((1, D_MODEL)), _resident(wu.shape), _resident(wd.shape),
                  _resident((1, D_MODEL)), _resident(wg.shape), _resident(wp.shape),
                  _resident((1, D_MODEL))],
        out_specs=row(D_MODEL),
        out_shape=jax.ShapeDtypeStruct((n, D_MODEL), F32),
        compiler_params=pltpu.CompilerParams(
            dimension_semantics=("arbitrary",), vmem_limit_bytes=VMEM_LIMIT),
        name=name,
    )(x, att, sga, gyc, p, wa, wo, vec(gf), wu, wd, vec(gp), wg, wp, vec(gl))


def _causal_bias(rows, cols, period):
    r = lax.broadcasted_iota(jnp.int32, (rows, cols), 0) % period
    c = lax.broadcasted_iota(jnp.int32, (rows, cols), 1)
    return jnp.where(c < r, 0.0, -jnp.inf).astype(F32)


def kernel(x_prompt, x_sample, p_prompt, p_sample, cache_k, cache_v, cache_conv,
           g_mix, w_in, conv_w, w_attn_out, w_conv_out, w_o,
           g_ffn, w_up, w_down, g_ple, w_ple_gate, w_ple, g_final):
    depth = w_in.shape[0]
    assert depth == 1
    b, t, _ = x_prompt.shape
    db, dt, _ = x_sample.shape
    past = cache_k.shape[2]
    bf = lambda w: w[0].astype(BF16)
    w_in_b, wa, wc, wo = bf(w_in), bf(w_attn_out), bf(w_conv_out), bf(w_o)
    wu, wd, wg, wp = bf(w_up), bf(w_down), bf(w_ple_gate), bf(w_ple)
    r = lax.broadcasted_iota(jnp.int32, (ATT_BLOCK, ATT_BLOCK), 0)
    c = lax.broadcasted_iota(jnp.int32, (ATT_BLOCK, ATT_BLOCK), 1)
    tri = (r >= c).astype(BF16)
    hd = (N_HEADS, HEAD_DIM)

    xp = x_prompt.reshape(b * t, D_MODEL)
    kpt, vpt, q_bf, kt_bf, v_bf, sga, gyc, conv_p = _proj_call(
        xp, g_mix[0], w_in_b, conv_w[0], wc, t)
    att = _attn_prompt_call(q_bf, kt_bf, v_bf, tri,
                            _causal_bias(HEADS_PER_TILE * KEY_TILE, KEY_TILE, KEY_TILE), b, t)
    yp = _post_call(xp, att, sga, gyc, p_prompt[0].reshape(b * t, PLE_DIM),
                    wa, wo, g_ffn[0], wu, wd, g_ple[0], wg, wp, g_final, "post_prompt")
    kp = kpt.reshape(b, *hd, t).transpose(0, 3, 1, 2)
    vp = vpt.reshape(b, *hd, t).transpose(0, 3, 1, 2)

    xs = x_sample.reshape(db * dt, D_MODEL)
    buf = cache_conv[0]
    zeros = jnp.zeros((db, dt, CONV_WIDTH), F32)
    e1 = zeros.at[:, 0].set(buf[:, 1]).reshape(db * dt, CONV_WIDTH)
    e2 = zeros.at[:, 0].set(buf[:, 0]).at[:, 1].set(buf[:, 1]).reshape(db * dt, CONV_WIDTH)
    ks, vs, q_bf, k_bf, v_bf, sga, gyc, u_s = _proj_call(
        xs, g_mix[0], w_in_b, conv_w[0], wc, dt, hist=(e1, e2))
    ckt = cache_k[0].transpose(0, 2, 3, 1).reshape(db, SB_WIDTH, past)
    cvt = cache_v[0].transpose(0, 2, 3, 1).reshape(db, SB_WIDTH, past)
    att = _attn_sample_call(q_bf, k_bf, v_bf, ckt, cvt, tri,
                            _causal_bias(HEADS_PER_TILE * dt, ATT_BLOCK, dt), db, dt)
    ys = _post_call(xs, att, sga, gyc, p_sample[0].reshape(db * dt, PLE_DIM),
                    wa, wo, g_ffn[0], wu, wd, g_ple[0], wg, wp, g_final, "post_sample")
    conv_s = u_s.reshape(db, dt, CONV_WIDTH)[:, dt - (CONV_K - 1):]

    return (yp.reshape(b, t, D_MODEL), ys.reshape(db, dt, D_MODEL),
            kp[None], vp[None], conv_p[None],
            ks.reshape(1, db, dt, *hd), vs.reshape(1, db, dt, *hd), conv_s[None])
```

```python
import functools
import math

import jax
import jax.numpy as jnp
from jax import lax
from jax.experimental import pallas as pl
from jax.experimental.pallas import tpu as pltpu

F32 = jnp.float32
BF16 = jnp.bfloat16

D_MODEL = 1024
N_HEADS = 8
HEAD_DIM = 64
SB_WIDTH = N_HEADS * HEAD_DIM
CONV_WIDTH = 512
CONV_K = 3
PLE_DIM = 256
D_FF = 4 * D_MODEL
EPS = 1e-6
N_PROJ = 3 * SB_WIDTH + 3 * CONV_WIDTH + 2 * D_MODEL

LANES = 128
HEADS_PER_TILE = LANES // HEAD_DIM
N_PAIRS = N_HEADS // HEADS_PER_TILE
PROJ_ROWS = 1024
POST_ROWS = 512
ATT_BLOCK = 256
ATT_ROWS = 512
SAMPLE_GROUP = 4
FF_CHUNK = 1024
POST_GROUPS = 2
CONV_PAD = 8
LOG2E = 1.4426950408889634
LOG2_ZERO = -126.0
KEY_TILE = LANES
VMEM_LIMIT = 54 * 1024 * 1024


def _resident(shape):
    nd = len(shape)
    return pl.BlockSpec(shape, lambda *_: (0,) * nd, pipeline_mode=pl.Buffered(1))


def _rms(x, g):
    return (x * lax.rsqrt(jnp.mean(x * x, axis=-1, keepdims=True) + EPS)) * g


def _dot(a, b):
    return jnp.dot(a, b, preferred_element_type=F32)


def _dot_nt(a, b):
    return lax.dot_general(a, b, (((1,), (1,)), ((), ())), preferred_element_type=F32)


def _proj_kernel(*refs, tiles_per_seq, seq_in_tile):
    prompt = seq_in_tile is None
    if prompt:
        (x_ref, g_ref, w_ref, cw_ref, wc_ref,
         k_out, v_out, q_bf, k_bf, v_bf, sga, gyc, cstate, s_ref) = refs
    else:
        (x_ref, g_ref, w_ref, cw_ref, wc_ref, e1_ref, e2_ref,
         k_out, v_out, q_bf, k_bf, v_bf, sga, gyc, cstate, s_ref) = refs
    tm = x_ref.shape[0]
    hb = _rms(x_ref[...], g_ref[...]).astype(BF16)
    o_q, o_k, o_v = 0, SB_WIDTH, 2 * SB_WIDTH
    o_cb = 3 * SB_WIDTH
    o_cc, o_cx = o_cb + CONV_WIDTH, o_cb + 2 * CONV_WIDTH
    o_ga = o_cb + 3 * CONV_WIDTH
    o_gc = o_ga + D_MODEL

    def proj(lo, width):
        return _dot(hb, w_ref[:, lo:lo + width])

    cb = proj(o_cb, CONV_WIDTH)
    u = proj(o_cc, CONV_WIDTH) * proj(o_cx, CONV_WIDTH)
    if prompt:
        @pl.when(pl.program_id(0) % tiles_per_seq == 0)
        def _():
            s_ref[0:CONV_PAD, :] = jnp.zeros((CONV_PAD, CONV_WIDTH), F32)
    else:
        s_ref[0:CONV_PAD, :] = jnp.zeros((CONV_PAD, CONV_WIDTH), F32)
    s_ref[CONV_PAD:CONV_PAD + tm, :] = u
    prev1 = s_ref[CONV_PAD - 1:CONV_PAD - 1 + tm, :]
    prev2 = s_ref[CONV_PAD - 2:CONV_PAD - 2 + tm, :]
    if prompt:
        s_ref[CONV_PAD - 2:CONV_PAD, :] = u[tm - 2:tm, :]
        cstate[0] = u[tm - 2:tm, :]
    else:
        r = lax.broadcasted_iota(jnp.int32, (tm, 1), 0) % seq_in_tile
        prev1 = jnp.where(r < 1, e1_ref[...], prev1)
        prev2 = jnp.where(r < 2, e2_ref[...], prev2)
        cstate[...] = u
    conv = cw_ref[0:1, :] * prev2 + cw_ref[1:2, :] * prev1 + cw_ref[2:3, :] * u
    cpre = (cb * conv).astype(BF16)

    sga[...] = jax.nn.sigmoid(proj(o_ga, D_MODEL)).astype(BF16)
    gyc[...] = (jax.nn.sigmoid(proj(o_gc, D_MODEL)) * _dot(cpre, wc_ref[...])).astype(BF16)

    k = proj(o_k, SB_WIDTH)
    v = proj(o_v, SB_WIDTH)
    if prompt:
        kt = k.T
        k_out[0] = kt
        for j in range(tm // KEY_TILE):
            k_bf[0, j] = kt[:, j * KEY_TILE:(j + 1) * KEY_TILE].astype(BF16)
        v_out[0] = v.T
    else:
        k_out[...] = k
        k_bf[...] = k.astype(BF16)
        v_out[...] = v
    v_bf[...] = v.astype(BF16)
    qb = (proj(o_q, SB_WIDTH) * (LOG2E / math.sqrt(HEAD_DIM))).astype(BF16)
    if prompt:
        first = lax.broadcasted_iota(jnp.int32, (1, SB_WIDTH), 1) % LANES < HEAD_DIM
        zero = jnp.zeros((), BF16)
        for j in range(tm // KEY_TILE):
            rows = qb[j * KEY_TILE:(j + 1) * KEY_TILE]
            base = j * HEADS_PER_TILE * KEY_TILE
            q_bf[base:base + KEY_TILE, :] = jnp.where(first, rows, zero)
            q_bf[base + KEY_TILE:base + 2 * KEY_TILE, :] = jnp.where(first, zero, rows)
    else:
        q_bf[...] = qb


def _proj_call(x, g, w_in, conv_w, w_conv_out, seq_len, hist=None):
    n = x.shape[0]
    tm = min(PROJ_ROWS, n)
    grid = (n // tm,)
    sample = hist is not None
    sds = jax.ShapeDtypeStruct
    row = lambda w: pl.BlockSpec((tm, w), lambda t: (t, 0))
    in_specs = [row(D_MODEL), _resident((1, D_MODEL)), _resident((D_MODEL, N_PROJ)),
                _resident((CONV_K, CONV_WIDTH)), _resident((CONV_WIDTH, D_MODEL))]
    args = [x, g.reshape(1, D_MODEL), w_in, conv_w, w_conv_out]
    if sample:
        in_specs += [row(CONV_WIDTH), row(CONV_WIDTH)]
        args += list(hist)
        tiles_per_seq, seq_in_tile = None, seq_len
        kv_shape = sds((n, SB_WIDTH), F32)
        kv_spec = row(SB_WIDTH)
        kbf_shape = sds((n, SB_WIDTH), BF16)
        kbf_spec = row(SB_WIDTH)
        cstate_shape = sds((n, CONV_WIDTH), F32)
        cstate_spec = row(CONV_WIDTH)
    else:
        tps = seq_len // tm
        nb = tm // KEY_TILE
        tiles_per_seq, seq_in_tile = tps, None
        batch = n // seq_len
        kv_shape = sds((batch, SB_WIDTH, seq_len), F32)
        kv_spec = pl.BlockSpec((1, SB_WIDTH, tm), lambda t: (t // tps, 0, t % tps))
        kbf_shape = sds((batch, seq_len // KEY_TILE, SB_WIDTH, KEY_TILE), BF16)
        kbf_spec = pl.BlockSpec((1, nb, SB_WIDTH, KEY_TILE), lambda t: (t // tps, t % tps, 0, 0))
        cstate_shape = sds((batch, CONV_K - 1, CONV_WIDTH), F32)
        cstate_spec = pl.BlockSpec((1, CONV_K - 1, CONV_WIDTH), lambda t: (t // tps, 0, 0))
    q_rows = 1 if sample else HEADS_PER_TILE
    out_shape = (kv_shape, kv_shape,
                 sds((q_rows * n, SB_WIDTH), BF16), kbf_shape, sds((n, SB_WIDTH), BF16),
                 sds((n, D_MODEL), BF16), sds((n, D_MODEL), BF16), cstate_shape)
    out_specs = (kv_spec, kv_spec, pl.BlockSpec((q_rows * tm, SB_WIDTH), lambda t: (t, 0)),
                 kbf_spec, row(SB_WIDTH), row(D_MODEL), row(D_MODEL), cstate_spec)
    return pl.pallas_call(
        functools.partial(_proj_kernel, tiles_per_seq=tiles_per_seq, seq_in_tile=seq_in_tile),
        grid=grid, in_specs=in_specs, out_specs=out_specs, out_shape=out_shape,
        scratch_shapes=[pltpu.VMEM((tm + CONV_PAD, CONV_WIDTH), F32)],
        compiler_params=pltpu.CompilerParams(
            dimension_semantics=("arbitrary",), vmem_limit_bytes=VMEM_LIMIT),
        name="proj_sample" if sample else "proj_prompt",
    )(*args)


def _softplus2(z):
    return jnp.maximum(z, 0.0) + jnp.log(1.0 + jnp.exp2(-jnp.abs(z))) * LOG2E


def _finish_block(z, vw, tri, spent, v_is_t=False):
    tail = _dot(_softplus2(z).astype(BF16), tri)
    a = jnp.exp2(z - tail - spent).astype(BF16)
    pv = _dot_nt(a, vw) if v_is_t else _dot(a, vw)
    return pv, spent + tail[:, 0:1]


def _stack_heads(qp):
    lane = lax.broadcasted_iota(jnp.int32, (1, LANES), 1)
    zero = jnp.zeros((), BF16)
    return jnp.concatenate([jnp.where(lane < HEAD_DIM, qp, zero),
                            jnp.where(lane >= HEAD_DIM, qp, zero)], axis=0)


def _unstack_heads(acc):
    lane = lax.broadcasted_iota(jnp.int32, (1, LANES), 1)
    half = acc.shape[0] // 2
    return jnp.where(lane < HEAD_DIM, acc[:half], acc[half:])


def _pair(p):
    return slice(p * LANES, (p + 1) * LANES)


def _attn_prompt_kernel(q_ref, kt_ref, v_ref, tri_ref, bias_ref, o_ref, spent_ref, acc_ref):
    i = pl.program_id(1)
    rows = HEADS_PER_TILE * KEY_TILE
    n_sub = q_ref.shape[0] // rows
    inst = [(p, s) for p in range(N_PAIRS) for s in range(n_sub)]

    def q2_of(p, s):
        return q_ref[s * rows:(s + 1) * rows, _pair(p)]

    def logits(p, s, jr, right_bias, jr_may_be_zero):
        jl = jnp.maximum(jr - 1, 0)
        kw = jnp.concatenate([kt_ref[0, jl, _pair(p), :], kt_ref[0, jr, _pair(p), :]], axis=1)
        z = _dot(q2_of(p, s), kw)
        zl, zr = z[:, :KEY_TILE], z[:, KEY_TILE:]
        if jr_may_be_zero:
            zl = zl + jnp.where(jr >= 1, 0.0, -jnp.inf)
        if right_bias is not None:
            zr = zr + right_bias
        rows_of = lambda j: pl.ds(pl.multiple_of(j * KEY_TILE, KEY_TILE), KEY_TILE)
        vw = jnp.concatenate([v_ref[rows_of(jl), _pair(p)], v_ref[rows_of(jr), _pair(p)]], axis=0)
        return jnp.concatenate([zl, zr], axis=1), vw

    zs, vws = [], []
    for p, s in inst:
        z, vw = logits(p, s, n_sub * i + s, bias_ref[...], s == 0)
        zs.append(z)
        vws.append(vw)
    sps = jnp.concatenate([_softplus2(z).astype(BF16) for z in zs], axis=0)
    tails = _dot(sps, tri_ref[...])
    least = []
    for n, (p, s) in enumerate(inst):
        tail = tails[n * rows:(n + 1) * rows]
        a = jnp.exp2(zs[n] - tail).astype(BF16)
        acc_ref[p, s] = _dot(a, vws[n])
        spent_ref[p, s] = tail[:, 0:1]
        least.append(jnp.min(tail[:, 0:1]))

    for n, (p, s) in enumerate(inst):
        def body(state, p=p, s=s):
            jr, _ = state
            z, vw = logits(p, s, jr, None, True)
            pv, spent = _finish_block(z, vw, tri_ref[...], spent_ref[p, s])
            acc_ref[p, s] += pv
            spent_ref[p, s] = spent
            return jr - 2, jnp.min(spent)

        lax.while_loop(lambda st: (st[0] >= 0) & (st[1] < -LOG2_ZERO), body,
                       (n_sub * i + s - 2, least[n]))

    for p, s in inst:
        o_ref[s * KEY_TILE:(s + 1) * KEY_TILE, _pair(p)] = _unstack_heads(acc_ref[p, s]).astype(BF16)


def _attn_prompt_call(q_bf, kt_bf, v_bf, tri, bias, batch, seq):
    tq = ATT_ROWS
    nq = seq // tq
    n_sub = tq // KEY_TILE
    rows = HEADS_PER_TILE * KEY_TILE
    return pl.pallas_call(
        _attn_prompt_kernel,
        grid=(batch, nq),
        in_specs=[pl.BlockSpec((HEADS_PER_TILE * tq, SB_WIDTH), lambda b, i: (b * nq + i, 0)),
                  pl.BlockSpec((1, seq // KEY_TILE, SB_WIDTH, KEY_TILE), lambda b, i: (b, 0, 0, 0)),
                  pl.BlockSpec((seq, SB_WIDTH), lambda b, i: (b, 0)),
                  _resident(tri.shape), _resident(bias.shape)],
        out_specs=pl.BlockSpec((tq, SB_WIDTH), lambda b, i: (b * nq + i, 0)),
        out_shape=jax.ShapeDtypeStruct((batch * seq, SB_WIDTH), BF16),
        scratch_shapes=[pltpu.VMEM((N_PAIRS, n_sub, rows, 1), F32),
                        pltpu.VMEM((N_PAIRS, n_sub, rows, LANES), F32)],
        compiler_params=pltpu.CompilerParams(
            dimension_semantics=("arbitrary", "arbitrary"), vmem_limit_bytes=VMEM_LIMIT),
        name="attn_prompt",
    )(q_bf, kt_bf, v_bf, tri, bias)


def _attn_sample_kernel(q_ref, kn_ref, vn_ref, ck0_ref, cv0_ref, ck_any, cv_any, tri_ref, bias_ref,
                        o_ref, spent_ref, acc_ref, kpad_ref, vpad_ref, kbuf_ref, vbuf_ref, sem, *, seq):
    g = pl.program_id(0)
    n_el = q_ref.shape[0] // seq
    tk = tri_ref.shape[0]
    n_win = ck_any.shape[2] // tk
    rows = HEADS_PER_TILE * seq
    inst = [(e, p) for e in range(n_el) for p in range(N_PAIRS)]
    el = lambda e: slice(e * seq, (e + 1) * seq)
    kpad_ref[...] = jnp.zeros_like(kpad_ref)
    vpad_ref[...] = jnp.zeros_like(vpad_ref)
    for e in range(n_el):
        kpad_ref[e, 0:seq, :] = kn_ref[el(e), :]
        vpad_ref[e, 0:seq, :] = vn_ref[el(e), :]
    zs = []
    for e, p in inst:
        q2 = _stack_heads(q_ref[el(e), _pair(p)])
        zs.append(_dot_nt(q2, kpad_ref[e, :, _pair(p)]) + bias_ref[...])
        zs.append(_dot(q2, ck0_ref[e, _pair(p), :].astype(BF16)))
    tails = _dot(jnp.concatenate([_softplus2(z).astype(BF16) for z in zs], axis=0), tri_ref[...])
    least = []
    for n, (e, p) in enumerate(inst):
        t_new = tails[(2 * n) * rows:(2 * n + 1) * rows]
        t_old = tails[(2 * n + 1) * rows:(2 * n + 2) * rows]
        a_new = jnp.exp2(zs[2 * n] - t_new).astype(BF16)
        a_old = jnp.exp2(zs[2 * n + 1] - t_old - t_new[:, 0:1]).astype(BF16)
        acc_ref[e, p] = (_dot(a_new, vpad_ref[e, :, _pair(p)])
                         + _dot_nt(a_old, cv0_ref[e, _pair(p), :].astype(BF16)))
        spent = t_new[:, 0:1] + t_old[:, 0:1]
        spent_ref[e, p] = spent
        least.append(jnp.min(spent))
    for n, (e, p) in enumerate(inst):
        def window(src, dst, j, slot, e=e, p=p):
            k0 = pl.multiple_of(j * tk, tk)
            return pltpu.make_async_copy(src.at[g * n_el + e, _pair(p), pl.ds(k0, tk)], dst, sem.at[slot])

        def body(state, e=e, p=p, window=window):
            j, _ = state
            window(ck_any, kbuf_ref, j, 0).start()
            window(cv_any, vbuf_ref, j, 1).start()
            window(ck_any, kbuf_ref, j, 0).wait()
            window(cv_any, vbuf_ref, j, 1).wait()
            z = _dot(_stack_heads(q_ref[el(e), _pair(p)]), kbuf_ref[...].astype(BF16))
            pv, spent = _finish_block(z, vbuf_ref[...].astype(BF16), tri_ref[...], spent_ref[e, p], True)
            acc_ref[e, p] += pv
            spent_ref[e, p] = spent
            return j - 1, jnp.min(spent)

        lax.while_loop(lambda s: (s[0] >= 0) & (s[1] < -LOG2_ZERO), body, (n_win - 2, least[n]))
    for e, p in inst:
        o_ref[el(e), _pair(p)] = _unstack_heads(acc_ref[e, p]).astype(BF16)


def _attn_sample_call(q_bf, k_bf, v_bf, ckt, cvt, tri, bias, batch, seq):
    past = ckt.shape[2]
    tk = tri.shape[0]
    n_win = past // tk
    n_el = SAMPLE_GROUP
    rows = HEADS_PER_TILE * seq
    row = pl.BlockSpec((n_el * seq, SB_WIDTH), lambda g: (g, 0))
    last = pl.BlockSpec((n_el, SB_WIDTH, tk), lambda g: (g, 0, n_win - 1))
    hbm = pl.BlockSpec(memory_space=pl.ANY)
    return pl.pallas_call(
        functools.partial(_attn_sample_kernel, seq=seq),
        grid=(batch // n_el,),
        in_specs=[row, row, row, last, last, hbm, hbm, _resident(tri.shape), _resident(bias.shape)],
        out_specs=row,
        out_shape=jax.ShapeDtypeStruct((batch * seq, SB_WIDTH), BF16),
        scratch_shapes=[pltpu.VMEM((n_el, N_PAIRS, rows, 1), F32),
                        pltpu.VMEM((n_el, N_PAIRS, rows, LANES), F32),
                        pltpu.VMEM((n_el, tk, SB_WIDTH), BF16), pltpu.VMEM((n_el, tk, SB_WIDTH), BF16),
                        pltpu.VMEM((LANES, tk), F32), pltpu.VMEM((LANES, tk), F32),
                        pltpu.SemaphoreType.DMA((2,))],
        compiler_params=pltpu.CompilerParams(
            dimension_semantics=("arbitrary",), vmem_limit_bytes=VMEM_LIMIT),
        name="attn_sample",
    )(q_bf, k_bf, v_bf, ckt, cvt, ckt, cvt, tri, bias)


def _post_kernel(*refs, prompt_steps):
    prompt, sample, weights, (yp_ref, ys_ref) = refs[0:5], refs[5:10], refs[10:19], refs[19:21]
    on_sample = pl.program_id(0) >= prompt_steps

    @pl.when(jnp.logical_not(on_sample))
    def _():
        _post_body(*prompt, *weights, yp_ref)

    @pl.when(on_sample)
    def _():
        _post_body(*sample, *weights, ys_ref)


def _post_body(x_ref, att_ref, sga_ref, gyc_ref, p_ref,
               wa_ref, wo_ref, gf_ref, wu_ref, wd_ref, gp_ref, wg_ref, wp_ref, gl_ref,
               y_ref):
    tm = x_ref.shape[0]
    groups = [slice(r0, r0 + tm // POST_GROUPS) for r0 in range(0, tm, tm // POST_GROUPS)]
    each = lambda fn, *cols: [fn(*vals) for vals in zip(*cols)]
    ya = each(lambda rs: _dot(att_ref[rs], wa_ref[...]), groups)
    merged = each(lambda rs, ya: sga_ref[rs].astype(F32) * ya + gyc_ref[rs].astype(F32), groups, ya)
    x1 = each(lambda rs, m: x_ref[rs] + _dot(m.astype(BF16), wo_ref[...]), groups, merged)
    h2 = each(lambda x: _rms(x, gf_ref[...]).astype(BF16), x1)
    f = each(jnp.zeros_like, x1)
    for c in range(0, D_FF, FF_CHUNK):
        up = each(lambda h: jnp.maximum(_dot(h, wu_ref[:, c:c + FF_CHUNK]), 0.0), h2)
        f = each(lambda f, u: f + _dot((u * u).astype(BF16), wd_ref[c:c + FF_CHUNK, :]), f, up)
    x2 = each(lambda a, b: a + b, x1, f)
    h3 = each(lambda x: _rms(x, gp_ref[...]).astype(BF16), x2)
    gate = each(lambda h: jax.nn.sigmoid(_dot(h, wg_ref[...])), h3)
    x3 = each(lambda rs, x, g: x + g * _dot(p_ref[rs].astype(BF16), wp_ref[...]), groups, x2, gate)
    for rs, x in zip(groups, x3):
        y_ref[rs] = _rms(x, gl_ref[...])


def _post_call(prompt, sample, wa, wo, gf, wu, wd, gp, wg, wp, gl):
    n_p, n_s = prompt[0].shape[0], sample[0].shape[0]
    tm = POST_ROWS
    steps_p, steps_s = n_p // tm, n_s // tm
    widths = (D_MODEL, SB_WIDTH, D_MODEL, D_MODEL, PLE_DIM)
    p_row = lambda w: pl.BlockSpec((tm, w), lambda t: (jnp.minimum(t, steps_p - 1), 0))
    s_map = lambda t: (jnp.maximum(t - steps_p, 0), 0)
    s_in = lambda w: pl.BlockSpec((tm, w), s_map, pipeline_mode=pl.Buffered(1))
    vec = lambda g: g.reshape(1, D_MODEL)
    return pl.pallas_call(
        functools.partial(_post_kernel, prompt_steps=steps_p),
        grid=(steps_p + steps_s,),
        in_specs=[p_row(w) for w in widths] + [s_in(w) for w in widths] + [
            _resident(wa.shape), _resident(wo.shape),
            _resident((1, D_MODEL)), _resident(wu.shape), _resident(wd.shape),
            _resident((1, D_MODEL)), _resident(wg.shape), _resident(wp.shape),
            _resident((1, D_MODEL))],
        out_specs=(p_row(D_MODEL), pl.BlockSpec((tm, D_MODEL), s_map)),
        out_shape=(jax.ShapeDtypeStruct((n_p, D_MODEL), F32),
                   jax.ShapeDtypeStruct((n_s, D_MODEL), F32)),
        compiler_params=pltpu.CompilerParams(
            dimension_semantics=("arbitrary",), vmem_limit_bytes=VMEM_LIMIT),
        name="post",
    )(*prompt, *sample, wa, wo, vec(gf), wu, wd, vec(gp), wg, wp, vec(gl))


def _causal_bias(rows, cols, period):
    r = lax.broadcasted_iota(jnp.int32, (rows, cols), 0) % period
    c = lax.broadcasted_iota(jnp.int32, (rows, cols), 1)
    return jnp.where(c < r, 0.0, -jnp.inf).astype(F32)


def kernel(x_prompt, x_sample, p_prompt, p_sample, cache_k, cache_v, cache_conv,
           g_mix, w_in, conv_w, w_attn_out, w_conv_out, w_o,
           g_ffn, w_up, w_down, g_ple, w_ple_gate, w_ple, g_final):
    depth = w_in.shape[0]
    assert depth == 1
    b, t, _ = x_prompt.shape
    db, dt, _ = x_sample.shape
    past = cache_k.shape[2]
    bf = lambda w: w[0].astype(BF16)
    w_in_b, wa, wc, wo = bf(w_in), bf(w_attn_out), bf(w_conv_out), bf(w_o)
    wu, wd, wg, wp = bf(w_up), bf(w_down), bf(w_ple_gate), bf(w_ple)
    r = lax.broadcasted_iota(jnp.int32, (ATT_BLOCK, ATT_BLOCK), 0)
    c = lax.broadcasted_iota(jnp.int32, (ATT_BLOCK, ATT_BLOCK), 1)
    tri = (r >= c).astype(BF16)
    hd = (N_HEADS, HEAD_DIM)

    xp = x_prompt.reshape(b * t, D_MODEL)
    kpt, vpt, q_bf, kt_bf, v_bf, sga, gyc, conv_p = _proj_call(
        xp, g_mix[0], w_in_b, conv_w[0], wc, t)
    att = _attn_prompt_call(q_bf, kt_bf, v_bf, tri,
                            _causal_bias(HEADS_PER_TILE * KEY_TILE, KEY_TILE, KEY_TILE), b, t)
    prompt = (xp, att, sga, gyc, p_prompt[0].reshape(b * t, PLE_DIM))
    kp = kpt.reshape(b, *hd, t).transpose(0, 3, 1, 2)
    vp = vpt.reshape(b, *hd, t).transpose(0, 3, 1, 2)

    xs = x_sample.reshape(db * dt, D_MODEL)
    buf = cache_conv[0]
    zeros = jnp.zeros((db, dt, CONV_WIDTH), F32)
    e1 = zeros.at[:, 0].set(buf[:, 1]).reshape(db * dt, CONV_WIDTH)
    e2 = zeros.at[:, 0].set(buf[:, 0]).at[:, 1].set(buf[:, 1]).reshape(db * dt, CONV_WIDTH)
    ks, vs, q_bf, k_bf, v_bf, sga, gyc, u_s = _proj_call(
        xs, g_mix[0], w_in_b, conv_w[0], wc, dt, hist=(e1, e2))
    ckt = cache_k[0].transpose(0, 2, 3, 1).reshape(db, SB_WIDTH, past)
    cvt = cache_v[0].transpose(0, 2, 3, 1).reshape(db, SB_WIDTH, past)
    att = _attn_sample_call(q_bf, k_bf, v_bf, ckt, cvt, tri,
                            _causal_bias(HEADS_PER_TILE * dt, ATT_BLOCK, dt), db, dt)
    sample = (xs, att, sga, gyc, p_sample[0].reshape(db * dt, PLE_DIM))
    conv_s = u_s.reshape(db, dt, CONV_WIDTH)[:, dt - (CONV_K - 1):]

    yp, ys = _post_call(prompt, sample, wa, wo, g_ffn[0], wu, wd, g_ple[0], wg, wp, g_final)

    return (yp.reshape(b, t, D_MODEL), ys.reshape(db, dt, D_MODEL),
            kp[None], vp[None], conv_p[None],
            ks.reshape(1, db, dt, *hd), vs.reshape(1, db, dt, *hd), conv_s[None])
```

```python
import functools
import math

import jax
import jax.numpy as jnp
from jax import lax
from jax.experimental import pallas as pl
from jax.experimental.pallas import tpu as pltpu

F32 = jnp.float32
BF16 = jnp.bfloat16

D_MODEL = 1024
N_HEADS = 8
HEAD_DIM = 64
SB_WIDTH = N_HEADS * HEAD_DIM
CONV_WIDTH = 512
CONV_K = 3
PLE_DIM = 256
D_FF = 4 * D_MODEL
EPS = 1e-6
N_PROJ = 3 * SB_WIDTH + 3 * CONV_WIDTH + 2 * D_MODEL

LANES = 128
HEADS_PER_TILE = LANES // HEAD_DIM
N_PAIRS = N_HEADS // HEADS_PER_TILE
PROJ_ROWS = 1024
SAMPLE_PROJ_ROWS = 256
POST_ROWS = 512
KEY_TILE = LANES
ATT_BLOCK = 2 * KEY_TILE
ATT_ROWS = 1024
SAMPLE_GROUP = 8
FF_CHUNK = 1024
POST_GROUPS = 2
CONV_PAD = 8
LOG2E = 1.4426950408889634
LOG2_ZERO = -126.0
VMEM_LIMIT = 52 * 1024 * 1024


def _resident(shape):
    nd = len(shape)
    return pl.BlockSpec(shape, lambda *_: (0,) * nd, pipeline_mode=pl.Buffered(1))


def _rms(x, g):
    return (x * lax.rsqrt(jnp.mean(x * x, axis=-1, keepdims=True) + EPS)) * g


def _dot(a, b):
    return jnp.dot(a, b, preferred_element_type=F32)


def _dot_nt(a, b):
    return lax.dot_general(a, b, (((1,), (1,)), ((), ())), preferred_element_type=F32)


def _proj_kernel(*refs, tiles_per_seq, seq_in_tile):
    prompt = seq_in_tile is None
    if prompt:
        (x_ref, g_ref, w_ref, cw_ref, wc_ref,
         k_out, v_out, q_bf, k_bf, v_bf, sga, gyc, cstate, s_ref) = refs
    else:
        (x_ref, g_ref, w_ref, cw_ref, wc_ref, e1_ref, e2_ref,
         k_out, v_out, q_bf, k_bf, v_bf, sga, gyc, cstate, s_ref) = refs
    tm = x_ref.shape[0]
    hb = _rms(x_ref[...], g_ref[...]).astype(BF16)
    o_q, o_k, o_v = 0, SB_WIDTH, 2 * SB_WIDTH
    o_cb = 3 * SB_WIDTH
    o_cc, o_cx = o_cb + CONV_WIDTH, o_cb + 2 * CONV_WIDTH
    o_ga = o_cb + 3 * CONV_WIDTH
    o_gc = o_ga + D_MODEL

    def proj(lo, width):
        return _dot(hb, w_ref[:, lo:lo + width])

    cb = proj(o_cb, CONV_WIDTH)
    u = proj(o_cc, CONV_WIDTH) * proj(o_cx, CONV_WIDTH)
    if prompt:
        @pl.when(pl.program_id(0) % tiles_per_seq == 0)
        def _():
            s_ref[0:CONV_PAD, :] = jnp.zeros((CONV_PAD, CONV_WIDTH), F32)
    else:
        s_ref[0:CONV_PAD, :] = jnp.zeros((CONV_PAD, CONV_WIDTH), F32)
    s_ref[CONV_PAD:CONV_PAD + tm, :] = u
    prev1 = s_ref[CONV_PAD - 1:CONV_PAD - 1 + tm, :]
    prev2 = s_ref[CONV_PAD - 2:CONV_PAD - 2 + tm, :]
    if prompt:
        s_ref[CONV_PAD - 2:CONV_PAD, :] = u[tm - 2:tm, :]
        cstate[0] = u[tm - 2:tm, :]
    else:
        r = lax.broadcasted_iota(jnp.int32, (tm, 1), 0) % seq_in_tile
        prev1 = jnp.where(r < 1, e1_ref[...], prev1)
        prev2 = jnp.where(r < 2, e2_ref[...], prev2)
        cstate[...] = u
    conv = cw_ref[0:1, :] * prev2 + cw_ref[1:2, :] * prev1 + cw_ref[2:3, :] * u
    cpre = (cb * conv).astype(BF16)

    sga[...] = jax.nn.sigmoid(proj(o_ga, D_MODEL)).astype(BF16)
    gyc[...] = (jax.nn.sigmoid(proj(o_gc, D_MODEL)) * _dot(cpre, wc_ref[...])).astype(BF16)

    k = proj(o_k, SB_WIDTH)
    v = proj(o_v, SB_WIDTH)
    if prompt:
        kt = k.T
        k_out[0] = kt
        for j in range(tm // KEY_TILE):
            k_bf[0, j] = kt[:, j * KEY_TILE:(j + 1) * KEY_TILE].astype(BF16)
        v_out[0] = v.T
    else:
        k_out[...] = k
        k_bf[...] = k.astype(BF16)
        v_out[...] = v
    v_bf[...] = v.astype(BF16)
    qb = (proj(o_q, SB_WIDTH) * (LOG2E / math.sqrt(HEAD_DIM))).astype(BF16)
    if prompt:
        first = lax.broadcasted_iota(jnp.int32, (1, SB_WIDTH), 1) % LANES < HEAD_DIM
        zero = jnp.zeros((), BF16)
        for j in range(tm // KEY_TILE):
            rows = qb[j * KEY_TILE:(j + 1) * KEY_TILE]
            base = j * HEADS_PER_TILE * KEY_TILE
            q_bf[base:base + KEY_TILE, :] = jnp.where(first, rows, zero)
            q_bf[base + KEY_TILE:base + 2 * KEY_TILE, :] = jnp.where(first, zero, rows)
    else:
        q_bf[...] = qb


def _proj_call(x, g, w_in, conv_w, w_conv_out, seq_len, hist=None):
    n = x.shape[0]
    sample = hist is not None
    tm = SAMPLE_PROJ_ROWS if sample else PROJ_ROWS
    grid = (n // tm,)
    sds = jax.ShapeDtypeStruct
    row = lambda w: pl.BlockSpec((tm, w), lambda t: (t, 0))
    in_specs = [row(D_MODEL), _resident((1, D_MODEL)), _resident((D_MODEL, N_PROJ)),
                _resident((CONV_K, CONV_WIDTH)), _resident((CONV_WIDTH, D_MODEL))]
    args = [x, g.reshape(1, D_MODEL), w_in, conv_w, w_conv_out]
    if sample:
        in_specs += [row(CONV_WIDTH), row(CONV_WIDTH)]
        args += list(hist)
        tiles_per_seq, seq_in_tile = None, seq_len
        kv_shape = sds((n, SB_WIDTH), F32)
        kv_spec = row(SB_WIDTH)
        kbf_shape = sds((n, SB_WIDTH), BF16)
        kbf_spec = row(SB_WIDTH)
        cstate_shape = sds((n, CONV_WIDTH), F32)
        cstate_spec = row(CONV_WIDTH)
    else:
        tps = seq_len // tm
        nb = tm // KEY_TILE
        tiles_per_seq, seq_in_tile = tps, None
        batch = n // seq_len
        kv_shape = sds((batch, SB_WIDTH, seq_len), F32)
        kv_spec = pl.BlockSpec((1, SB_WIDTH, tm), lambda t: (t // tps, 0, t % tps))
        kbf_shape = sds((batch, seq_len // KEY_TILE, SB_WIDTH, KEY_TILE), BF16)
        kbf_spec = pl.BlockSpec((1, nb, SB_WIDTH, KEY_TILE), lambda t: (t // tps, t % tps, 0, 0))
        cstate_shape = sds((batch, CONV_K - 1, CONV_WIDTH), F32)
        cstate_spec = pl.BlockSpec((1, CONV_K - 1, CONV_WIDTH), lambda t: (t // tps, 0, 0))
    q_rows = 1 if sample else HEADS_PER_TILE
    out_shape = (kv_shape, kv_shape,
                 sds((q_rows * n, SB_WIDTH), BF16), kbf_shape, sds((n, SB_WIDTH), BF16),
                 sds((n, D_MODEL), BF16), sds((n, D_MODEL), BF16), cstate_shape)
    out_specs = (kv_spec, kv_spec, pl.BlockSpec((q_rows * tm, SB_WIDTH), lambda t: (t, 0)),
                 kbf_spec, row(SB_WIDTH), row(D_MODEL), row(D_MODEL), cstate_spec)
    return pl.pallas_call(
        functools.partial(_proj_kernel, tiles_per_seq=tiles_per_seq, seq_in_tile=seq_in_tile),
        grid=grid, in_specs=in_specs, out_specs=out_specs, out_shape=out_shape,
        scratch_shapes=[pltpu.VMEM((tm + CONV_PAD, CONV_WIDTH), F32)],
        compiler_params=pltpu.CompilerParams(
            dimension_semantics=("arbitrary",), vmem_limit_bytes=VMEM_LIMIT),
        name="proj_sample" if sample else "proj_prompt",
    )(*args)


def _softplus2(z):
    return jnp.maximum(z, 0.0) + jnp.log(1.0 + jnp.exp2(-jnp.abs(z))) * LOG2E


def _finish_block(z, vw, tri, spent, v_is_t=False):
    tail = _dot(_softplus2(z).astype(BF16), tri)
    a = jnp.exp2(z - tail - spent).astype(BF16)
    pv = _dot_nt(a, vw) if v_is_t else _dot(a, vw)
    return pv, spent + tail[:, 0:1]


def _stack_heads(qp):
    lane = lax.broadcasted_iota(jnp.int32, (1, LANES), 1)
    zero = jnp.zeros((), BF16)
    return jnp.concatenate([jnp.where(lane < HEAD_DIM, qp, zero),
                            jnp.where(lane >= HEAD_DIM, qp, zero)], axis=0)


def _unstack_heads(acc):
    lane = lax.broadcasted_iota(jnp.int32, (1, LANES), 1)
    half = acc.shape[0] // 2
    return jnp.where(lane < HEAD_DIM, acc[:half], acc[half:])


def _pair(p):
    return slice(p * LANES, (p + 1) * LANES)


def _attn_prompt_kernel(q_ref, kt_any, v_any, tri_ref, bias_ref, o_ref,
                        spent_ref, acc_ref, kt_ref, v_ref, sem):
    b, i = pl.program_id(0), pl.program_id(1)
    nq = pl.num_programs(1)
    rows = HEADS_PER_TILE * KEY_TILE
    n_sub = q_ref.shape[0] // rows
    inst = [(p, s) for p in range(N_PAIRS) for s in range(n_sub)]
    tq = n_sub * KEY_TILE
    seq = nq * tq

    def chunk(step, part):
        bb, ii = step // nq, step % nq
        slot = step % 2
        if part == 0:
            tiles = pl.ds(ii * n_sub, n_sub)
            return pltpu.make_async_copy(kt_any.at[bb, tiles], kt_ref.at[tiles], sem.at[slot, 0])
        src = pl.ds(pl.multiple_of(bb * seq + ii * tq, tq), tq)
        dst = pl.ds(pl.multiple_of(ii * tq, tq), tq)
        return pltpu.make_async_copy(v_any.at[src], v_ref.at[dst], sem.at[slot, 1])

    step = b * nq + i

    @pl.when(step == 0)
    def _():
        chunk(step, 0).start()
        chunk(step, 1).start()

    @pl.when(i + 1 < nq)
    def _():
        chunk(step + 1, 0).start()
        chunk(step + 1, 1).start()

    chunk(step, 0).wait()
    chunk(step, 1).wait()

    def q2_of(p, s):
        return q_ref[s * rows:(s + 1) * rows, _pair(p)]

    def logits(p, s, jr, right_bias, jr_may_be_zero):
        jl = jnp.maximum(jr - 1, 0)
        kw = jnp.concatenate([kt_ref[jl, _pair(p), :], kt_ref[jr, _pair(p), :]], axis=1)
        z = _dot(q2_of(p, s), kw)
        zl, zr = z[:, :KEY_TILE], z[:, KEY_TILE:]
        if jr_may_be_zero:
            zl = zl + jnp.where(jr >= 1, 0.0, -jnp.inf)
        if right_bias is not None:
            zr = zr + right_bias
        rows_of = lambda j: pl.ds(pl.multiple_of(j * KEY_TILE, KEY_TILE), KEY_TILE)
        vw = jnp.concatenate([v_ref[rows_of(jl), _pair(p)], v_ref[rows_of(jr), _pair(p)]], axis=0)
        return jnp.concatenate([zl, zr], axis=1), vw

    zs, vws = [], []
    for p, s in inst:
        z, vw = logits(p, s, n_sub * i + s, bias_ref[...], s == 0)
        zs.append(z)
        vws.append(vw)
    sps = jnp.concatenate([_softplus2(z).astype(BF16) for z in zs], axis=0)
    tails = _dot(sps, tri_ref[...])
    least = []
    for n, (p, s) in enumerate(inst):
        tail = tails[n * rows:(n + 1) * rows]
        a = jnp.exp2(zs[n] - tail).astype(BF16)
        acc_ref[p, s] = _dot(a, vws[n])
        spent_ref[p, s] = tail[:, 0:1]
        least.append(jnp.min(tail[:, 0:1]))

    for n, (p, s) in enumerate(inst):
        def body(state, p=p, s=s):
            jr, _ = state
            z, vw = logits(p, s, jr, None, True)
            pv, spent = _finish_block(z, vw, tri_ref[...], spent_ref[p, s])
            acc_ref[p, s] += pv
            spent_ref[p, s] = spent
            return jr - 2, jnp.min(spent)

        lax.while_loop(lambda st: (st[0] >= 0) & (st[1] < -LOG2_ZERO), body,
                       (n_sub * i + s - 2, least[n]))

    for p, s in inst:
        o_ref[s * KEY_TILE:(s + 1) * KEY_TILE, _pair(p)] = _unstack_heads(acc_ref[p, s]).astype(BF16)

    @pl.when((i + 1 == nq) & (b + 1 < pl.num_programs(0)))
    def _():
        chunk(step + 1, 0).start()
        chunk(step + 1, 1).start()


def _attn_prompt_call(q_bf, kt_bf, v_bf, tri, bias, batch, seq):
    tq = ATT_ROWS
    nq = seq // tq
    n_sub = tq // KEY_TILE
    rows = HEADS_PER_TILE * KEY_TILE
    return pl.pallas_call(
        _attn_prompt_kernel,
        grid=(batch, nq),
        in_specs=[pl.BlockSpec((HEADS_PER_TILE * tq, SB_WIDTH), lambda b, i: (b * nq + i, 0)),
                  pl.BlockSpec(memory_space=pl.ANY), pl.BlockSpec(memory_space=pl.ANY),
                  _resident(tri.shape), _resident(bias.shape)],
        out_specs=pl.BlockSpec((tq, SB_WIDTH), lambda b, i: (b * nq + i, 0)),
        out_shape=jax.ShapeDtypeStruct((batch * seq, SB_WIDTH), BF16),
        scratch_shapes=[pltpu.VMEM((N_PAIRS, n_sub, rows, 1), F32),
                        pltpu.VMEM((N_PAIRS, n_sub, rows, LANES), F32),
                        pltpu.VMEM((seq // KEY_TILE, SB_WIDTH, KEY_TILE), BF16),
                        pltpu.VMEM((seq, SB_WIDTH), BF16),
                        pltpu.SemaphoreType.DMA((2, 2))],
        compiler_params=pltpu.CompilerParams(
            dimension_semantics=("arbitrary", "arbitrary"), vmem_limit_bytes=VMEM_LIMIT),
        name="attn_prompt",
    )(q_bf, kt_bf, v_bf, tri, bias)


def _attn_sample_kernel(q_ref, kn_ref, vn_ref, ck0_ref, cv0_ref, ck_any, cv_any, tri_ref, bias_ref,
                        o_ref, spent_ref, acc_ref, kpad_ref, vpad_ref, kbuf_ref, vbuf_ref, sem, *, seq):
    g = pl.program_id(0)
    n_el = q_ref.shape[0] // seq
    tk = tri_ref.shape[0]
    n_win = ck_any.shape[2] // tk
    rows = HEADS_PER_TILE * seq
    inst = [(e, p) for e in range(n_el) for p in range(N_PAIRS)]
    el = lambda e: slice(e * seq, (e + 1) * seq)
    kpad_ref[...] = jnp.zeros_like(kpad_ref)
    vpad_ref[...] = jnp.zeros_like(vpad_ref)
    for e in range(n_el):
        kpad_ref[e, 0:seq, :] = kn_ref[el(e), :]
        vpad_ref[e, 0:seq, :] = vn_ref[el(e), :]
    zs = []
    for e, p in inst:
        q2 = _stack_heads(q_ref[el(e), _pair(p)])
        zs.append(_dot_nt(q2, kpad_ref[e, :, _pair(p)]) + bias_ref[...])
        zs.append(_dot(q2, ck0_ref[e, _pair(p), :].astype(BF16)))
    tails = _dot(jnp.concatenate([_softplus2(z).astype(BF16) for z in zs], axis=0), tri_ref[...])
    least = []
    for n, (e, p) in enumerate(inst):
        t_new = tails[(2 * n) * rows:(2 * n + 1) * rows]
        t_old = tails[(2 * n + 1) * rows:(2 * n + 2) * rows]
        a_new = jnp.exp2(zs[2 * n] - t_new).astype(BF16)
        a_old = jnp.exp2(zs[2 * n + 1] - t_old - t_new[:, 0:1]).astype(BF16)
        acc_ref[e, p] = (_dot(a_new, vpad_ref[e, :, _pair(p)])
                         + _dot_nt(a_old, cv0_ref[e, _pair(p), :].astype(BF16)))
        spent = t_new[:, 0:1] + t_old[:, 0:1]
        spent_ref[e, p] = spent
        least.append(jnp.min(spent))
    for n, (e, p) in enumerate(inst):
        def window(src, dst, j, slot, e=e, p=p):
            k0 = pl.multiple_of(j * tk, tk)
            return pltpu.make_async_copy(src.at[g * n_el + e, _pair(p), pl.ds(k0, tk)], dst, sem.at[slot])

        def body(state, e=e, p=p, window=window):
            j, _ = state
            window(ck_any, kbuf_ref, j, 0).start()
            window(cv_any, vbuf_ref, j, 1).start()
            window(ck_any, kbuf_ref, j, 0).wait()
            window(cv_any, vbuf_ref, j, 1).wait()
            z = _dot(_stack_heads(q_ref[el(e), _pair(p)]), kbuf_ref[...].astype(BF16))
            pv, spent = _finish_block(z, vbuf_ref[...].astype(BF16), tri_ref[...], spent_ref[e, p], True)
            acc_ref[e, p] += pv
            spent_ref[e, p] = spent
            return j - 1, jnp.min(spent)

        lax.while_loop(lambda s: (s[0] >= 0) & (s[1] < -LOG2_ZERO), body, (n_win - 2, least[n]))
    for e, p in inst:
        o_ref[el(e), _pair(p)] = _unstack_heads(acc_ref[e, p]).astype(BF16)


def _attn_sample_call(q_bf, k_bf, v_bf, ckt, cvt, tri, bias, batch, seq):
    past = ckt.shape[2]
    tk = tri.shape[0]
    n_win = past // tk
    n_el = SAMPLE_GROUP
    rows = HEADS_PER_TILE * seq
    row = pl.BlockSpec((n_el * seq, SB_WIDTH), lambda g: (g, 0))
    last = pl.BlockSpec((n_el, SB_WIDTH, tk), lambda g: (g, 0, n_win - 1))
    hbm = pl.BlockSpec(memory_space=pl.ANY)
    return pl.pallas_call(
        functools.partial(_attn_sample_kernel, seq=seq),
        grid=(batch // n_el,),
        in_specs=[row, row, row, last, last, hbm, hbm, _resident(tri.shape), _resident(bias.shape)],
        out_specs=row,
        out_shape=jax.ShapeDtypeStruct((batch * seq, SB_WIDTH), BF16),
        scratch_shapes=[pltpu.VMEM((n_el, N_PAIRS, rows, 1), F32),
                        pltpu.VMEM((n_el, N_PAIRS, rows, LANES), F32),
                        pltpu.VMEM((n_el, tk, SB_WIDTH), BF16), pltpu.VMEM((n_el, tk, SB_WIDTH), BF16),
                        pltpu.VMEM((LANES, tk), F32), pltpu.VMEM((LANES, tk), F32),
                        pltpu.SemaphoreType.DMA((2,))],
        compiler_params=pltpu.CompilerParams(
            dimension_semantics=("arbitrary",), vmem_limit_bytes=VMEM_LIMIT),
        name="attn_sample",
    )(q_bf, k_bf, v_bf, ckt, cvt, ckt, cvt, tri, bias)


def _post_kernel(x_ref, att_ref, sga_ref, gyc_ref, p_ref,
                 wa_ref, wo_ref, gf_ref, wu_ref, wd_ref, gp_ref, wg_ref, wp_ref, gl_ref,
                 y_ref):
    tm = x_ref.shape[0]
    groups = [slice(r0, r0 + tm // POST_GROUPS) for r0 in range(0, tm, tm // POST_GROUPS)]
    each = lambda fn, *cols: [fn(*vals) for vals in zip(*cols)]
    ya = each(lambda rs: _dot(att_ref[rs], wa_ref[...]), groups)
    merged = each(lambda rs, ya: sga_ref[rs].astype(F32) * ya + gyc_ref[rs].astype(F32), groups, ya)
    x1 = each(lambda rs, m: x_ref[rs] + _dot(m.astype(BF16), wo_ref[...]), groups, merged)
    h2 = each(lambda x: _rms(x, gf_ref[...]).astype(BF16), x1)
    f = each(jnp.zeros_like, x1)
    for c in range(0, D_FF, FF_CHUNK):
        up = each(lambda h: jnp.maximum(_dot(h, wu_ref[:, c:c + FF_CHUNK]), 0.0), h2)
        f = each(lambda f, u: f + _dot((u * u).astype(BF16), wd_ref[c:c + FF_CHUNK, :]), f, up)
    x2 = each(lambda a, b: a + b, x1, f)
    h3 = each(lambda x: _rms(x, gp_ref[...]).astype(BF16), x2)
    gate = each(lambda h: jax.nn.sigmoid(_dot(h, wg_ref[...])), h3)
    x3 = each(lambda rs, x, g: x + g * _dot(p_ref[rs].astype(BF16), wp_ref[...]), groups, x2, gate)
    for rs, x in zip(groups, x3):
        y_ref[rs] = _rms(x, gl_ref[...])


def _post_call(x, att, sga, gyc, p, wa, wo, gf, wu, wd, gp, wg, wp, gl, name):
    n = x.shape[0]
    tm = min(POST_ROWS, n)
    row = lambda w: pl.BlockSpec((tm, w), lambda t: (t, 0))
    vec = lambda g: g.reshape(1, D_MODEL)
    return pl.pallas_call(
        _post_kernel,
        grid=(n // tm,),
        in_specs=[row(D_MODEL), row(SB_WIDTH), row(D_MODEL), row(D_MODEL), row(PLE_DIM),
                  _resident(wa.shape), _resident(wo.shape),
                  _resident((1, D_MODEL)), _resident(wu.shape), _resident(wd.shape),
                  _resident((1, D_MODEL)), _resident(wg.shape), _resident(wp.shape),
                  _resident((1, D_MODEL))],
        out_specs=row(D_MODEL),
        out_shape=jax.ShapeDtypeStruct((n, D_MODEL), F32),
        compiler_params=pltpu.CompilerParams(
            dimension_semantics=("arbitrary",), vmem_limit_bytes=VMEM_LIMIT),
        name=name,
    )(x, att, sga, gyc, p, wa, wo, vec(gf), wu, wd, vec(gp), wg, wp, vec(gl))


def _causal_bias(rows, cols, period):
    r = lax.broadcasted_iota(jnp.int32, (rows, cols), 0) % period
    c = lax.broadcasted_iota(jnp.int32, (rows, cols), 1)
    return jnp.where(c < r, 0.0, -jnp.inf).astype(F32)


def kernel(x_prompt, x_sample, p_prompt, p_sample, cache_k, cache_v, cache_conv,
           g_mix, w_in, conv_w, w_attn_out, w_conv_out, w_o,
           g_ffn, w_up, w_down, g_ple, w_ple_gate, w_ple, g_final):
    depth = w_in.shape[0]
    assert depth == 1
    b, t, _ = x_prompt.shape
    db, dt, _ = x_sample.shape
    past = cache_k.shape[2]
    bf = lambda w: w[0].astype(BF16)
    w_in_b, wa, wc, wo = bf(w_in), bf(w_attn_out), bf(w_conv_out), bf(w_o)
    wu, wd, wg, wp = bf(w_up), bf(w_down), bf(w_ple_gate), bf(w_ple)
    r = lax.broadcasted_iota(jnp.int32, (ATT_BLOCK, ATT_BLOCK), 0)
    c = lax.broadcasted_iota(jnp.int32, (ATT_BLOCK, ATT_BLOCK), 1)
    tri = (r >= c).astype(BF16)
    hd = (N_HEADS, HEAD_DIM)

    xp = x_prompt.reshape(b * t, D_MODEL)
    kpt, vpt, q_bf, kt_bf, v_bf, sga, gyc, conv_p = _proj_call(
        xp, g_mix[0], w_in_b, conv_w[0], wc, t)
    att = _attn_prompt_call(q_bf, kt_bf, v_bf, tri,
                            _causal_bias(HEADS_PER_TILE * KEY_TILE, KEY_TILE, KEY_TILE), b, t)
    yp = _post_call(xp, att, sga, gyc, p_prompt[0].reshape(b * t, PLE_DIM),
                    wa, wo, g_ffn[0], wu, wd, g_ple[0], wg, wp, g_final, "post_prompt")
    kp = kpt.reshape(b, *hd, t).transpose(0, 3, 1, 2)
    vp = vpt.reshape(b, *hd, t).transpose(0, 3, 1, 2)

    xs = x_sample.reshape(db * dt, D_MODEL)
    buf = cache_conv[0]
    zeros = jnp.zeros((db, dt, CONV_WIDTH), F32)
    e1 = zeros.at[:, 0].set(buf[:, 1]).reshape(db * dt, CONV_WIDTH)
    e2 = zeros.at[:, 0].set(buf[:, 0]).at[:, 1].set(buf[:, 1]).reshape(db * dt, CONV_WIDTH)
    ks, vs, q_bf, k_bf, v_bf, sga, gyc, u_s = _proj_call(
        xs, g_mix[0], w_in_b, conv_w[0], wc, dt, hist=(e1, e2))
    ckt = cache_k[0].transpose(0, 2, 3, 1).reshape(db, SB_WIDTH, past)
    cvt = cache_v[0].transpose(0, 2, 3, 1).reshape(db, SB_WIDTH, past)
    att = _attn_sample_call(q_bf, k_bf, v_bf, ckt, cvt, tri,
                            _causal_bias(HEADS_PER_TILE * dt, ATT_BLOCK, dt), db, dt)
    ys = _post_call(xs, att, sga, gyc, p_sample[0].reshape(db * dt, PLE_DIM),
                    wa, wo, g_ffn[0], wu, wd, g_ple[0], wg, wp, g_final, "post_sample")
    conv_s = u_s.reshape(db, dt, CONV_WIDTH)[:, dt - (CONV_K - 1):]

    return (yp.reshape(b, t, D_MODEL), ys.reshape(db, dt, D_MODEL),
            kp[None], vp[None], conv_p[None],
            ks.reshape(1, db, dt, *hd), vs.reshape(1, db, dt, *hd), conv_s[None])
```

```python
import functools
import math

import jax
import jax.numpy as jnp
from jax import lax
from jax.experimental import pallas as pl
from jax.experimental.pallas import tpu as pltpu

F32 = jnp.float32
BF16 = jnp.bfloat16

D_MODEL = 1024
N_HEADS = 8
HEAD_DIM = 64
SB_WIDTH = N_HEADS * HEAD_DIM
CONV_WIDTH = 512
CONV_K = 3
PLE_DIM = 256
D_FF = 4 * D_MODEL
EPS = 1e-6
N_PROJ = 3 * SB_WIDTH + 3 * CONV_WIDTH + 2 * D_MODEL

LANES = 128
HEADS_PER_TILE = LANES // HEAD_DIM
N_PAIRS = N_HEADS // HEADS_PER_TILE
PROJ_ROWS = 1024
SAMPLE_PROJ_ROWS = 256
POST_ROWS = 512
KEY_TILE = LANES
ATT_BLOCK = 2 * KEY_TILE
ATT_ROWS = 1024
SAMPLE_GROUP = 8
FF_CHUNK = 1024
POST_GROUPS = 2
CONV_PAD = 8
LOG2E = 1.4426950408889634
LOG2_ZERO = -126.0
VMEM_LIMIT = 52 * 1024 * 1024


def _resident(shape):
    nd = len(shape)
    return pl.BlockSpec(shape, lambda *_: (0,) * nd, pipeline_mode=pl.Buffered(1))


def _rms(x, g):
    return (x * lax.rsqrt(jnp.mean(x * x, axis=-1, keepdims=True) + EPS)) * g


def _dot(a, b):
    return jnp.dot(a, b, preferred_element_type=F32)


def _dot_nt(a, b):
    return lax.dot_general(a, b, (((1,), (1,)), ((), ())), preferred_element_type=F32)


def _proj_kernel(*refs, tiles_per_seq, seq_in_tile):
    prompt = seq_in_tile is None
    if prompt:
        (x_ref, g_ref, w_ref, cw_ref, wc_ref,
         k_out, v_out, q_bf, k_bf, v_bf, sga, gyc, cstate, s_ref) = refs
    else:
        (x_ref, g_ref, w_ref, cw_ref, wc_ref, e1_ref, e2_ref,
         k_out, v_out, q_bf, k_bf, v_bf, sga, gyc, cstate, s_ref) = refs
    tm = x_ref.shape[0]
    hb = _rms(x_ref[...], g_ref[...]).astype(BF16)
    o_q, o_k, o_v = 0, SB_WIDTH, 2 * SB_WIDTH
    o_cb = 3 * SB_WIDTH
    o_cc, o_cx = o_cb + CONV_WIDTH, o_cb + 2 * CONV_WIDTH
    o_ga = o_cb + 3 * CONV_WIDTH
    o_gc = o_ga + D_MODEL

    def proj(lo, width):
        return _dot(hb, w_ref[:, lo:lo + width])

    cb = proj(o_cb, CONV_WIDTH)
    u = proj(o_cc, CONV_WIDTH) * proj(o_cx, CONV_WIDTH)
    if prompt:
        @pl.when(pl.program_id(0) % tiles_per_seq == 0)
        def _():
            s_ref[0:CONV_PAD, :] = jnp.zeros((CONV_PAD, CONV_WIDTH), F32)
    else:
        s_ref[0:CONV_PAD, :] = jnp.zeros((CONV_PAD, CONV_WIDTH), F32)
    s_ref[CONV_PAD:CONV_PAD + tm, :] = u
    prev1 = s_ref[CONV_PAD - 1:CONV_PAD - 1 + tm, :]
    prev2 = s_ref[CONV_PAD - 2:CONV_PAD - 2 + tm, :]
    if prompt:
        s_ref[CONV_PAD - 2:CONV_PAD, :] = u[tm - 2:tm, :]
        cstate[0] = u[tm - 2:tm, :]
    else:
        r = lax.broadcasted_iota(jnp.int32, (tm, 1), 0) % seq_in_tile
        prev1 = jnp.where(r < 1, e1_ref[...], prev1)
        prev2 = jnp.where(r < 2, e2_ref[...], prev2)
        cstate[...] = u
    conv = cw_ref[0:1, :] * prev2 + cw_ref[1:2, :] * prev1 + cw_ref[2:3, :] * u
    cpre = (cb * conv).astype(BF16)

    k = proj(o_k, SB_WIDTH)
    v = proj(o_v, SB_WIDTH)
    if prompt:
        kt = k.T
        k_out[0] = kt
        for j in range(tm // KEY_TILE):
            k_bf[0, j] = kt[:, j * KEY_TILE:(j + 1) * KEY_TILE].astype(BF16)
        v_out[0] = v.T
    else:
        k_out[...] = k
        k_bf[...] = k.astype(BF16)
        v_out[...] = v
    v_bf[...] = v.astype(BF16)
    qb = (proj(o_q, SB_WIDTH) * (LOG2E / math.sqrt(HEAD_DIM))).astype(BF16)
    if prompt:
        first = lax.broadcasted_iota(jnp.int32, (1, SB_WIDTH), 1) % LANES < HEAD_DIM
        zero = jnp.zeros((), BF16)
        for j in range(tm // KEY_TILE):
            rows = qb[j * KEY_TILE:(j + 1) * KEY_TILE]
            base = j * HEADS_PER_TILE * KEY_TILE
            q_bf[base:base + KEY_TILE, :] = jnp.where(first, rows, zero)
            q_bf[base + KEY_TILE:base + 2 * KEY_TILE, :] = jnp.where(first, zero, rows)
    else:
        q_bf[...] = qb

    top = jnp.max(cpre.astype(F32), axis=0, keepdims=True)[:, 0:1]
    word = lax.bitcast_convert_type(top, jnp.uint32)
    zero = lax.bitcast_convert_type((word >> 16) >> 16, F32).astype(BF16)
    hb = jnp.concatenate([hb[0:16] + zero, hb[16:]], axis=0)
    sga[...] = jax.nn.sigmoid(proj(o_ga, D_MODEL)).astype(BF16)
    gyc[...] = (jax.nn.sigmoid(proj(o_gc, D_MODEL)) * _dot(cpre, wc_ref[...])).astype(BF16)


def _proj_call(x, g, w_in, conv_w, w_conv_out, seq_len, hist=None):
    n = x.shape[0]
    sample = hist is not None
    tm = SAMPLE_PROJ_ROWS if sample else PROJ_ROWS
    grid = (n // tm,)
    sds = jax.ShapeDtypeStruct
    row = lambda w: pl.BlockSpec((tm, w), lambda t: (t, 0))
    in_specs = [row(D_MODEL), _resident((1, D_MODEL)), _resident((D_MODEL, N_PROJ)),
                _resident((CONV_K, CONV_WIDTH)), _resident((CONV_WIDTH, D_MODEL))]
    args = [x, g.reshape(1, D_MODEL), w_in, conv_w, w_conv_out]
    if sample:
        in_specs += [row(CONV_WIDTH), row(CONV_WIDTH)]
        args += list(hist)
        tiles_per_seq, seq_in_tile = None, seq_len
        kv_shape = sds((n, SB_WIDTH), F32)
        kv_spec = row(SB_WIDTH)
        kbf_shape = sds((n, SB_WIDTH), BF16)
        kbf_spec = row(SB_WIDTH)
        cstate_shape = sds((n, CONV_WIDTH), F32)
        cstate_spec = row(CONV_WIDTH)
    else:
        tps = seq_len // tm
        nb = tm // KEY_TILE
        tiles_per_seq, seq_in_tile = tps, None
        batch = n // seq_len
        kv_shape = sds((batch, SB_WIDTH, seq_len), F32)
        kv_spec = pl.BlockSpec((1, SB_WIDTH, tm), lambda t: (t // tps, 0, t % tps))
        kbf_shape = sds((batch, seq_len // KEY_TILE, SB_WIDTH, KEY_TILE), BF16)
        kbf_spec = pl.BlockSpec((1, nb, SB_WIDTH, KEY_TILE), lambda t: (t // tps, t % tps, 0, 0))
        cstate_shape = sds((batch, CONV_K - 1, CONV_WIDTH), F32)
        cstate_spec = pl.BlockSpec((1, CONV_K - 1, CONV_WIDTH), lambda t: (t // tps, 0, 0))
    q_rows = 1 if sample else HEADS_PER_TILE
    out_shape = (kv_shape, kv_shape,
                 sds((q_rows * n, SB_WIDTH), BF16), kbf_shape, sds((n, SB_WIDTH), BF16),
                 sds((n, D_MODEL), BF16), sds((n, D_MODEL), BF16), cstate_shape)
    out_specs = (kv_spec, kv_spec, pl.BlockSpec((q_rows * tm, SB_WIDTH), lambda t: (t, 0)),
                 kbf_spec, row(SB_WIDTH), row(D_MODEL), row(D_MODEL), cstate_spec)
    return pl.pallas_call(
        functools.partial(_proj_kernel, tiles_per_seq=tiles_per_seq, seq_in_tile=seq_in_tile),
        grid=grid, in_specs=in_specs, out_specs=out_specs, out_shape=out_shape,
        scratch_shapes=[pltpu.VMEM((tm + CONV_PAD, CONV_WIDTH), F32)],
        compiler_params=pltpu.CompilerParams(
            dimension_semantics=("arbitrary",), vmem_limit_bytes=VMEM_LIMIT),
        name="proj_sample" if sample else "proj_prompt",
    )(*args)


def _softplus2(z):
    return jnp.maximum(z, 0.0) + jnp.log(1.0 + jnp.exp2(-jnp.abs(z))) * LOG2E


def _finish_block(z, vw, tri, spent, v_is_t=False):
    tail = _dot(_softplus2(z).astype(BF16), tri)
    a = jnp.exp2(z - tail - spent).astype(BF16)
    pv = _dot_nt(a, vw) if v_is_t else _dot(a, vw)
    return pv, spent + tail[:, 0:1]


def _stack_heads(qp):
    lane = lax.broadcasted_iota(jnp.int32, (1, LANES), 1)
    zero = jnp.zeros((), BF16)
    return jnp.concatenate([jnp.where(lane < HEAD_DIM, qp, zero),
                            jnp.where(lane >= HEAD_DIM, qp, zero)], axis=0)


def _unstack_heads(acc):
    lane = lax.broadcasted_iota(jnp.int32, (1, LANES), 1)
    half = acc.shape[0] // 2
    return jnp.where(lane < HEAD_DIM, acc[:half], acc[half:])


def _pair(p):
    return slice(p * LANES, (p + 1) * LANES)


def _attn_prompt_kernel(q_ref, kt_any, v_any, tri_ref, bias_ref, o_ref,
                        spent_ref, acc_ref, kt_ref, v_ref, sem):
    b, i = pl.program_id(0), pl.program_id(1)
    nq = pl.num_programs(1)
    rows = HEADS_PER_TILE * KEY_TILE
    n_sub = q_ref.shape[0] // rows
    inst = [(p, s) for p in range(N_PAIRS) for s in range(n_sub)]
    tq = n_sub * KEY_TILE
    seq = nq * tq

    def chunk(step, part):
        bb, ii = step // nq, step % nq
        slot = step % 2
        if part == 0:
            tiles = pl.ds(ii * n_sub, n_sub)
            return pltpu.make_async_copy(kt_any.at[bb, tiles], kt_ref.at[tiles], sem.at[slot, 0])
        src = pl.ds(pl.multiple_of(bb * seq + ii * tq, tq), tq)
        dst = pl.ds(pl.multiple_of(ii * tq, tq), tq)
        return pltpu.make_async_copy(v_any.at[src], v_ref.at[dst], sem.at[slot, 1])

    step = b * nq + i

    @pl.when(step == 0)
    def _():
        chunk(step, 0).start()
        chunk(step, 1).start()

    @pl.when(i + 1 < nq)
    def _():
        chunk(step + 1, 0).start()
        chunk(step + 1, 1).start()

    chunk(step, 0).wait()
    chunk(step, 1).wait()

    def q2_of(p, s):
        return q_ref[s * rows:(s + 1) * rows, _pair(p)]

    def logits(p, s, jr, right_bias, jr_may_be_zero):
        jl = jnp.maximum(jr - 1, 0)
        kw = jnp.concatenate([kt_ref[jl, _pair(p), :], kt_ref[jr, _pair(p), :]], axis=1)
        z = _dot(q2_of(p, s), kw)
        zl, zr = z[:, :KEY_TILE], z[:, KEY_TILE:]
        if jr_may_be_zero:
            zl = zl + jnp.where(jr >= 1, 0.0, -jnp.inf)
        if right_bias is not None:
            zr = zr + right_bias
        rows_of = lambda j: pl.ds(pl.multiple_of(j * KEY_TILE, KEY_TILE), KEY_TILE)
        vw = jnp.concatenate([v_ref[rows_of(jl), _pair(p)], v_ref[rows_of(jr), _pair(p)]], axis=0)
        return jnp.concatenate([zl, zr], axis=1), vw

    zs, vws = [], []
    for p, s in inst:
        z, vw = logits(p, s, n_sub * i + s, bias_ref[...], s == 0)
        zs.append(z)
        vws.append(vw)
    sps = jnp.concatenate([_softplus2(z).astype(BF16) for z in zs], axis=0)
    tails = _dot(sps, tri_ref[...])
    least = []
    for n, (p, s) in enumerate(inst):
        tail = tails[n * rows:(n + 1) * rows]
        a = jnp.exp2(zs[n] - tail).astype(BF16)
        acc_ref[p, s] = _dot(a, vws[n])
        spent_ref[p, s] = tail[:, 0:1]
        least.append(jnp.min(tail[:, 0:1]))

    for n, (p, s) in enumerate(inst):
        def body(state, p=p, s=s):
            jr, _ = state
            z, vw = logits(p, s, jr, None, True)
            pv, spent = _finish_block(z, vw, tri_ref[...], spent_ref[p, s])
            acc_ref[p, s] += pv
            spent_ref[p, s] = spent
            return jr - 2, jnp.min(spent)

        lax.while_loop(lambda st: (st[0] >= 0) & (st[1] < -LOG2_ZERO), body,
                       (n_sub * i + s - 2, least[n]))

    for p, s in inst:
        o_ref[s * KEY_TILE:(s + 1) * KEY_TILE, _pair(p)] = _unstack_heads(acc_ref[p, s]).astype(BF16)

    @pl.when((i + 1 == nq) & (b + 1 < pl.num_programs(0)))
    def _():
        chunk(step + 1, 0).start()
        chunk(step + 1, 1).start()


def _attn_prompt_call(q_bf, kt_bf, v_bf, tri, bias, batch, seq):
    tq = ATT_ROWS
    nq = seq // tq
    n_sub = tq // KEY_TILE
    rows = HEADS_PER_TILE * KEY_TILE
    return pl.pallas_call(
        _attn_prompt_kernel,
        grid=(batch, nq),
        in_specs=[pl.BlockSpec((HEADS_PER_TILE * tq, SB_WIDTH), lambda b, i: (b * nq + i, 0)),
                  pl.BlockSpec(memory_space=pl.ANY), pl.BlockSpec(memory_space=pl.ANY),
                  _resident(tri.shape), _resident(bias.shape)],
        out_specs=pl.BlockSpec((tq, SB_WIDTH), lambda b, i: (b * nq + i, 0)),
        out_shape=jax.ShapeDtypeStruct((batch * seq, SB_WIDTH), BF16),
        scratch_shapes=[pltpu.VMEM((N_PAIRS, n_sub, rows, 1), F32),
                        pltpu.VMEM((N_PAIRS, n_sub, rows, LANES), F32),
                        pltpu.VMEM((seq // KEY_TILE, SB_WIDTH, KEY_TILE), BF16),
                        pltpu.VMEM((seq, SB_WIDTH), BF16),
                        pltpu.SemaphoreType.DMA((2, 2))],
        compiler_params=pltpu.CompilerParams(
            dimension_semantics=("arbitrary", "arbitrary"), vmem_limit_bytes=VMEM_LIMIT),
        name="attn_prompt",
    )(q_bf, kt_bf, v_bf, tri, bias)


def _attn_sample_kernel(q_ref, kn_ref, vn_ref, ck0_ref, cv0_ref, ck_any, cv_any, tri_ref, bias_ref,
                        o_ref, spent_ref, acc_ref, kpad_ref, vpad_ref, kbuf_ref, vbuf_ref, sem, *, seq):
    g = pl.program_id(0)
    n_el = q_ref.shape[0] // seq
    tk = tri_ref.shape[0]
    n_win = ck_any.shape[2] // tk
    rows = HEADS_PER_TILE * seq
    inst = [(e, p) for e in range(n_el) for p in range(N_PAIRS)]
    el = lambda e: slice(e * seq, (e + 1) * seq)
    kpad_ref[...] = jnp.zeros_like(kpad_ref)
    vpad_ref[...] = jnp.zeros_like(vpad_ref)
    for e in range(n_el):
        kpad_ref[e, 0:seq, :] = kn_ref[el(e), :]
        vpad_ref[e, 0:seq, :] = vn_ref[el(e), :]
    zs = []
    for e, p in inst:
        q2 = _stack_heads(q_ref[el(e), _pair(p)])
        zs.append(_dot_nt(q2, kpad_ref[e, :, _pair(p)]) + bias_ref[...])
        zs.append(_dot(q2, ck0_ref[e, _pair(p), :].astype(BF16)))
    tails = _dot(jnp.concatenate([_softplus2(z).astype(BF16) for z in zs], axis=0), tri_ref[...])
    least = []
    for n, (e, p) in enumerate(inst):
        t_new = tails[(2 * n) * rows:(2 * n + 1) * rows]
        t_old = tails[(2 * n + 1) * rows:(2 * n + 2) * rows]
        a_new = jnp.exp2(zs[2 * n] - t_new).astype(BF16)
        a_old = jnp.exp2(zs[2 * n + 1] - t_old - t_new[:, 0:1]).astype(BF16)
        acc_ref[e, p] = (_dot(a_new, vpad_ref[e, :, _pair(p)])
                         + _dot_nt(a_old, cv0_ref[e, _pair(p), :].astype(BF16)))
        spent = t_new[:, 0:1] + t_old[:, 0:1]
        spent_ref[e, p] = spent
        least.append(jnp.min(spent))
    for n, (e, p) in enumerate(inst):
        def window(src, dst, j, slot, e=e, p=p):
            k0 = pl.multiple_of(j * tk, tk)
            return pltpu.make_async_copy(src.at[g * n_el + e, _pair(p), pl.ds(k0, tk)], dst, sem.at[slot])

        def body(state, e=e, p=p, window=window):
            j, _ = state
            window(ck_any, kbuf_ref, j, 0).start()
            window(cv_any, vbuf_ref, j, 1).start()
            window(ck_any, kbuf_ref, j, 0).wait()
            window(cv_any, vbuf_ref, j, 1).wait()
            z = _dot(_stack_heads(q_ref[el(e), _pair(p)]), kbuf_ref[...].astype(BF16))
            pv, spent = _finish_block(z, vbuf_ref[...].astype(BF16), tri_ref[...], spent_ref[e, p], True)
            acc_ref[e, p] += pv
            spent_ref[e, p] = spent
            return j - 1, jnp.min(spent)

        lax.while_loop(lambda s: (s[0] >= 0) & (s[1] < -LOG2_ZERO), body, (n_win - 2, least[n]))
    for e, p in inst:
        o_ref[el(e), _pair(p)] = _unstack_heads(acc_ref[e, p]).astype(BF16)


def _attn_sample_call(q_bf, k_bf, v_bf, ckt, cvt, tri, bias, batch, seq):
    past = ckt.shape[2]
    tk = tri.shape[0]
    n_win = past // tk
    n_el = SAMPLE_GROUP
    rows = HEADS_PER_TILE * seq
    row = pl.BlockSpec((n_el * seq, SB_WIDTH), lambda g: (g, 0))
    last = pl.BlockSpec((n_el, SB_WIDTH, tk), lambda g: (g, 0, n_win - 1))
    hbm = pl.BlockSpec(memory_space=pl.ANY)
    return pl.pallas_call(
        functools.partial(_attn_sample_kernel, seq=seq),
        grid=(batch // n_el,),
        in_specs=[row, row, row, last, last, hbm, hbm, _resident(tri.shape), _resident(bias.shape)],
        out_specs=row,
        out_shape=jax.ShapeDtypeStruct((batch * seq, SB_WIDTH), BF16),
        scratch_shapes=[pltpu.VMEM((n_el, N_PAIRS, rows, 1), F32),
                        pltpu.VMEM((n_el, N_PAIRS, rows, LANES), F32),
                        pltpu.VMEM((n_el, tk, SB_WIDTH), BF16), pltpu.VMEM((n_el, tk, SB_WIDTH), BF16),
                        pltpu.VMEM((LANES, tk), F32), pltpu.VMEM((LANES, tk), F32),
                        pltpu.SemaphoreType.DMA((2,))],
        compiler_params=pltpu.CompilerParams(
            dimension_semantics=("arbitrary",), vmem_limit_bytes=VMEM_LIMIT),
        name="attn_sample",
    )(q_bf, k_bf, v_bf, ckt, cvt, ckt, cvt, tri, bias)


def _post_kernel(x_ref, att_ref, sga_ref, gyc_ref, p_ref,
                 wa_ref, wo_ref, gf_ref, wu_ref, wd_ref, gp_ref, wg_ref, wp_ref, gl_ref,
                 y_ref):
    tm = x_ref.shape[0]
    groups = [slice(r0, r0 + tm // POST_GROUPS) for r0 in range(0, tm, tm // POST_GROUPS)]
    each = lambda fn, *cols: [fn(*vals) for vals in zip(*cols)]
    ya = each(lambda rs: _dot(att_ref[rs], wa_ref[...]), groups)
    merged = each(lambda rs, ya: sga_ref[rs].astype(F32) * ya + gyc_ref[rs].astype(F32), groups, ya)
    x1 = each(lambda rs, m: x_ref[rs] + _dot(m.astype(BF16), wo_ref[...]), groups, merged)
    h2 = each(lambda x: _rms(x, gf_ref[...]).astype(BF16), x1)
    f = each(jnp.zeros_like, x1)
    for c in range(0, D_FF, FF_CHUNK):
        up = each(lambda h: jnp.maximum(_dot(h, wu_ref[:, c:c + FF_CHUNK]), 0.0), h2)
        f = each(lambda f, u: f + _dot((u * u).astype(BF16), wd_ref[c:c + FF_CHUNK, :]), f, up)
    pe = each(lambda rs: _dot(p_ref[rs].astype(BF16), wp_ref[...]), groups)
    x2 = each(lambda a, b: a + b, x1, f)
    h3 = each(lambda x: _rms(x, gp_ref[...]).astype(BF16), x2)
    gate = each(lambda h: jax.nn.sigmoid(_dot(h, wg_ref[...])), h3)
    x3 = each(lambda x, g, e: x + g * e, x2, gate, pe)
    for rs, x in zip(groups, x3):
        y_ref[rs] = _rms(x, gl_ref[...])


def _post_call(x, att, sga, gyc, p, wa, wo, gf, wu, wd, gp, wg, wp, gl, name):
    n = x.shape[0]
    tm = min(POST_ROWS, n)
    row = lambda w: pl.BlockSpec((tm, w), lambda t: (t, 0))
    vec = lambda g: g.reshape(1, D_MODEL)
    return pl.pallas_call(
        _post_kernel,
        grid=(n // tm,),
        in_specs=[row(D_MODEL), row(SB_WIDTH), row(D_MODEL), row(D_MODEL), row(PLE_DIM),
                  _resident(wa.shape), _resident(wo.shape),
                  _resident((1, D_MODEL)), _resident(wu.shape), _resident(wd.shape),
                  _resident((1, D_MODEL)), _resident(wg.shape), _resident(wp.shape),
                  _resident((1, D_MODEL))],
        out_specs=row(D_MODEL),
        out_shape=jax.ShapeDtypeStruct((n, D_MODEL), F32),
        compiler_params=pltpu.CompilerParams(
            dimension_semantics=("arbitrary",), vmem_limit_bytes=VMEM_LIMIT),
        name=name,
    )(x, att, sga, gyc, p, wa, wo, vec(gf), wu, wd, vec(gp), wg, wp, vec(gl))


def _causal_bias(rows, cols, period):
    r = lax.broadcasted_iota(jnp.int32, (rows, cols), 0) % period
    c = lax.broadcasted_iota(jnp.int32, (rows, cols), 1)
    return jnp.where(c < r, 0.0, -jnp.inf).astype(F32)


def kernel(x_prompt, x_sample, p_prompt, p_sample, cache_k, cache_v, cache_conv,
           g_mix, w_in, conv_w, w_attn_out, w_conv_out, w_o,
           g_ffn, w_up, w_down, g_ple, w_ple_gate, w_ple, g_final):
    depth = w_in.shape[0]
    assert depth == 1
    b, t, _ = x_prompt.shape
    db, dt, _ = x_sample.shape
    past = cache_k.shape[2]
    bf = lambda w: w[0].astype(BF16)
    w_in_b, wa, wc, wo = bf(w_in), bf(w_attn_out), bf(w_conv_out), bf(w_o)
    wu, wd, wg, wp = bf(w_up), bf(w_down), bf(w_ple_gate), bf(w_ple)
    r = lax.broadcasted_iota(jnp.int32, (ATT_BLOCK, ATT_BLOCK), 0)
    c = lax.broadcasted_iota(jnp.int32, (ATT_BLOCK, ATT_BLOCK), 1)
    tri = (r >= c).astype(BF16)
    hd = (N_HEADS, HEAD_DIM)

    xp = x_prompt.reshape(b * t, D_MODEL)
    kpt, vpt, q_bf, kt_bf, v_bf, sga, gyc, conv_p = _proj_call(
        xp, g_mix[0], w_in_b, conv_w[0], wc, t)
    att = _attn_prompt_call(q_bf, kt_bf, v_bf, tri,
                            _causal_bias(HEADS_PER_TILE * KEY_TILE, KEY_TILE, KEY_TILE), b, t)
    yp = _post_call(xp, att, sga, gyc, p_prompt[0].reshape(b * t, PLE_DIM),
                    wa, wo, g_ffn[0], wu, wd, g_ple[0], wg, wp, g_final, "post_prompt")
    kp = kpt.reshape(b, *hd, t).transpose(0, 3, 1, 2)
    vp = vpt.reshape(b, *hd, t).transpose(0, 3, 1, 2)

    xs = x_sample.reshape(db * dt, D_MODEL)
    buf = cache_conv[0]
    zeros = jnp.zeros((db, dt, CONV_WIDTH), F32)
    e1 = zeros.at[:, 0].set(buf[:, 1]).reshape(db * dt, CONV_WIDTH)
    e2 = zeros.at[:, 0].set(buf[:, 0]).at[:, 1].set(buf[:, 1]).reshape(db * dt, CONV_WIDTH)
    ks, vs, q_bf, k_bf, v_bf, sga, gyc, u_s = _proj_call(
        xs, g_mix[0], w_in_b, conv_w[0], wc, dt, hist=(e1, e2))
    ckt = cache_k[0].transpose(0, 2, 3, 1).reshape(db, SB_WIDTH, past)
    cvt = cache_v[0].transpose(0, 2, 3, 1).reshape(db, SB_WIDTH, past)
    att = _attn_sample_call(q_bf, k_bf, v_bf, ckt, cvt, tri,
                            _causal_bias(HEADS_PER_TILE * dt, ATT_BLOCK, dt), db, dt)
    ys = _post_call(xs, att, sga, gyc, p_sample[0].reshape(db * dt, PLE_DIM),
                    wa, wo, g_ffn[0], wu, wd, g_ple[0], wg, wp, g_final, "post_sample")
    conv_s = u_s.reshape(db, dt, CONV_WIDTH)[:, dt - (CONV_K - 1):]

    return (yp.reshape(b, t, D_MODEL), ys.reshape(db, dt, D_MODEL),
            kp[None], vp[None], conv_p[None],
            ks.reshape(1, db, dt, *hd), vs.reshape(1, db, dt, *hd), conv_s[None])
```
